```python
import math
import jax
import jax.numpy as jnp
from jax import lax
import numpy as np

D_MODEL = 2048
BATCH = 16
SEQ = 2048
DEPTH = 2

N_A = DEPTH // 2
N_B = DEPTH - N_A

EXPAND = 2
D_INNER = EXPAND * D_MODEL
SSM_HEAD_DIM = 64
N_SSM_HEADS = D_INNER // SSM_HEAD_DIM
D_STATE = 128
N_GROUPS = 8
HEADS_PER_GROUP = N_SSM_HEADS // N_GROUPS
CONV_W = 4
CONV_DIM = D_INNER + 2 * N_GROUPS * D_STATE
IN_DIM = D_INNER + CONV_DIM + N_SSM_HEADS
CHUNK = 128

DIFF_HEAD_DIM = 64
N_DIFF_HEADS = D_MODEL // (2 * DIFF_HEAD_DIM)
Q_BLOCK = 128

NUM_BUCKETS = 32
MAX_DISTANCE = 128

D_FF = 4 * D_MODEL

NORM_EPS = 1e-6

kernel_name = "yoco_mamba2_diffattn_block"


def rms_norm(x, w, eps=NORM_EPS):
    xf = x.astype(jnp.float32)
    y = xf * lax.rsqrt(jnp.mean(xf * xf, axis=-1, keepdims=True) + eps)
    return (y * w.astype(jnp.float32)).astype(x.dtype)


def sq_relu_mlp(u, w1, w2):
    return jnp.square(jax.nn.relu(u @ w1)) @ w2


def causal_depthwise_conv(x, w, b):
    out = lax.conv_general_dilated(
        x, w[:, None, :], window_strides=(1,), padding=[(CONV_W - 1, 0)],
        dimension_numbers=("NWC", "WIO", "NWC"), feature_group_count=x.shape[-1])
    return out + b


def ssd_chunked_scan(xs, dt, A, Bm, Cm):
    bsz, L = xs.shape[0], xs.shape[1]
    nc = L // CHUNK

    def to_chunks(t):
        return jnp.moveaxis(t.reshape((bsz, nc, CHUNK) + t.shape[2:]), 1, 0)

    causal = jnp.tril(jnp.ones((CHUNK, CHUNK), dtype=bool))[None, :, :, None, None]

    def step(state, inp):
        x, d, b, c = inp
        a = d * A
        cum = jnp.cumsum(a, axis=1)
        seg = cum[:, :, None] - cum[:, None, :]
        decay = jnp.where(causal, jnp.exp(jnp.where(causal, seg, 0.0)), 0.0)
        cb = jnp.einsum("bign,bjgn->bijg", c, b)
        y_intra = jnp.einsum("bijg,bijgr,bjgr,bjgrp->bigrp", cb, decay, d, x)
        y_inter = jnp.einsum("bign,bgrpn,bigr->bigrp", c, state, jnp.exp(cum))
        to_end = jnp.exp(cum[:, -1:] - cum) * d
        new_state = (state * jnp.exp(cum[:, -1])[..., None, None]
                     + jnp.einsum("bjgn,bjgr,bjgrp->bgrpn", b, to_end, x))
        return new_state, y_intra + y_inter

    state0 = jnp.zeros((bsz, N_GROUPS, HEADS_PER_GROUP, SSM_HEAD_DIM, D_STATE), jnp.float32)
    _, ys = lax.scan(step, state0, (to_chunks(xs), to_chunks(dt), to_chunks(Bm), to_chunks(Cm)))
    return jnp.moveaxis(ys, 0, 1).reshape(xs.shape)


def mamba2_mixer(u, in_proj, conv_w, conv_b, dt_bias, A_log, D_skip, gate_norm_w, out_proj):
    bsz, L, _ = u.shape
    zxbcdt = u @ in_proj
    z = zxbcdt[..., :D_INNER]
    xbc = zxbcdt[..., D_INNER:D_INNER + CONV_DIM]
    dt_raw = zxbcdt[..., D_INNER + CONV_DIM:]
    xbc = jax.nn.silu(causal_depthwise_conv(xbc, conv_w, conv_b))
    xs = xbc[..., :D_INNER].astype(jnp.float32).reshape(bsz, L, N_GROUPS, HEADS_PER_GROUP, SSM_HEAD_DIM)
    Bm = xbc[..., D_INNER:D_INNER + N_GROUPS * D_STATE].astype(jnp.float32).reshape(bsz, L, N_GROUPS, D_STATE)
    Cm = xbc[..., D_INNER + N_GROUPS * D_STATE:].astype(jnp.float32).reshape(bsz, L, N_GROUPS, D_STATE)
    dt = jax.nn.softplus(dt_raw.astype(jnp.float32) + dt_bias.astype(jnp.float32))
    dt = dt.reshape(bsz, L, N_GROUPS, HEADS_PER_GROUP)
    A = -jnp.exp(A_log.astype(jnp.float32)).reshape(N_GROUPS, HEADS_PER_GROUP)
    y = ssd_chunked_scan(xs, dt, A, Bm, Cm)
    y = y + D_skip.astype(jnp.float32).reshape(N_GROUPS, HEADS_PER_GROUP)[:, :, None] * xs
    y = y.reshape(bsz, L, D_INNER)
    g = y * jax.nn.silu(z.astype(jnp.float32))
    g = g.reshape(bsz, L, N_GROUPS, D_INNER // N_GROUPS)
    g = g * lax.rsqrt(jnp.mean(g * g, axis=-1, keepdims=True) + 1e-5)
    g = g.reshape(bsz, L, D_INNER) * gate_norm_w.astype(jnp.float32)
    return g.astype(u.dtype) @ out_proj


def t5_causal_bucket(dist):
    n = jnp.maximum(dist, 0)
    max_exact = NUM_BUCKETS // 2
    nf = jnp.maximum(n, 1).astype(jnp.float32)
    large = max_exact + (jnp.log(nf / max_exact) / math.log(MAX_DISTANCE / max_exact)
                         * (NUM_BUCKETS - max_exact)).astype(jnp.int32)
    large = jnp.minimum(large, NUM_BUCKETS - 1)
    return jnp.where(n < max_exact, n, large)


def diff_attention(u, k, v, w_q, lq1, lk1, lq2, lk2, subln_w, out_proj, rel_bias, lambda_init):
    bsz, S, _ = u.shape
    q = (u @ w_q).reshape(bsz, S, N_DIFF_HEADS, 2, DIFF_HEAD_DIM)
    lam = (jnp.exp(jnp.sum(lq1.astype(jnp.float32) * lk1.astype(jnp.float32)))
           - jnp.exp(jnp.sum(lq2.astype(jnp.float32) * lk2.astype(jnp.float32)))
           + lambda_init)
    scale = DIFF_HEAD_DIM ** -0.5
    kpos = jnp.arange(S, dtype=jnp.int32)

    def block(i):
        start = i * Q_BLOCK
        qb = lax.dynamic_slice_in_dim(q, start, Q_BLOCK, axis=1)
        logits = jnp.einsum("bqhcd,bkhcd->bhcqk", qb, k).astype(jnp.float32) * scale
        qpos = start + jnp.arange(Q_BLOCK, dtype=jnp.int32)
        dist = qpos[:, None] - kpos[None, :]
        bias = jnp.transpose(rel_bias[t5_causal_bucket(dist)], (2, 0, 1))
        logits = logits + bias.astype(jnp.float32)[None, :, None]
        logits = jnp.where((dist >= 0)[None, None, None], logits, -jnp.inf)
        p = jax.nn.softmax(logits, axis=-1)
        attn = p[:, :, 0] - lam * p[:, :, 1]
        return jnp.einsum("bhqk,bkhe->bqhe", attn.astype(v.dtype), v)

    o = lax.map(block, jnp.arange(S // Q_BLOCK))
    o = jnp.moveaxis(o, 0, 1).reshape(bsz, S, N_DIFF_HEADS, 2 * DIFF_HEAD_DIM)
    o = rms_norm(o, subln_w, eps=1e-5) * (1.0 - lambda_init)
    return o.reshape(bsz, S, D_MODEL) @ out_proj


def setup_inputs(seed: int = 0) -> dict:
    key = jax.random.key(seed)
    ks = jax.random.split(key, 32)
    f32 = jnp.float32

    def nrm(k, shape, fan_in):
        return jax.random.normal(k, shape, f32) * (fan_in ** -0.5)

    def gain(k, shape):
        return 1.0 + 0.05 * jax.random.normal(k, shape, f32)

    x = jax.random.normal(ks[0], (BATCH, SEQ, D_MODEL), f32)
    a_in_proj = nrm(ks[1], (N_A, D_MODEL, IN_DIM), D_MODEL)
    a_conv_w = nrm(ks[2], (N_A, CONV_W, CONV_DIM), CONV_W)
    a_conv_b = 0.01 * jax.random.normal(ks[3], (N_A, CONV_DIM), f32)
    dt0 = jnp.exp(jax.random.uniform(ks[4], (N_A, N_SSM_HEADS), f32, math.log(1e-3), math.log(1e-1)))
    a_dt_bias = dt0 + jnp.log(-jnp.expm1(-dt0))
    a_A_log = jnp.log(jax.random.uniform(ks[5], (N_A, N_SSM_HEADS), f32, 1.0, 16.0))
    a_D = gain(ks[6], (N_A, N_SSM_HEADS))
    a_gate_norm_w = gain(ks[7], (N_A, D_INNER))
    a_out_proj = nrm(ks[8], (N_A, D_INNER, D_MODEL), D_INNER)
    kv_norm_w = gain(ks[9], (D_MODEL,))
    w_k = nrm(ks[10], (D_MODEL, N_DIFF_HEADS * 2 * DIFF_HEAD_DIM), D_MODEL)
    w_v = nrm(ks[11], (D_MODEL, N_DIFF_HEADS * 2 * DIFF_HEAD_DIM), D_MODEL)
    rel_bias = 0.5 * jax.random.normal(ks[12], (NUM_BUCKETS, N_DIFF_HEADS), f32)
    b_w_q = nrm(ks[13], (N_B, D_MODEL, N_DIFF_HEADS * 2 * DIFF_HEAD_DIM), D_MODEL)
    b_lambda_q1 = 0.1 * jax.random.normal(ks[14], (N_B, DIFF_HEAD_DIM), f32)
    b_lambda_k1 = 0.1 * jax.random.normal(ks[15], (N_B, DIFF_HEAD_DIM), f32)
    b_lambda_q2 = 0.1 * jax.random.normal(ks[16], (N_B, DIFF_HEAD_DIM), f32)
    b_lambda_k2 = 0.1 * jax.random.normal(ks[17], (N_B, DIFF_HEAD_DIM), f32)
    b_subln_w = gain(ks[18], (N_B, 2 * DIFF_HEAD_DIM))
    b_out_proj = nrm(ks[19], (N_B, D_MODEL, D_MODEL), D_MODEL)
    norm_pre_mix = gain(ks[20], (DEPTH, D_MODEL))
    norm_post_mix = gain(ks[21], (DEPTH, D_MODEL))
    norm_pre_mlp = gain(ks[22], (DEPTH, D_MODEL))
    norm_post_mlp = gain(ks[23], (DEPTH, D_MODEL))
    mlp_w1 = nrm(ks[24], (DEPTH, D_MODEL, D_FF), D_MODEL)
    mlp_w2 = nrm(ks[25], (DEPTH, D_FF, D_MODEL), D_FF)
    return {
        "x": x,
        "a_in_proj": a_in_proj, "a_conv_w": a_conv_w, "a_conv_b": a_conv_b,
        "a_dt_bias": a_dt_bias, "a_A_log": a_A_log, "a_D": a_D,
        "a_gate_norm_w": a_gate_norm_w, "a_out_proj": a_out_proj,
        "kv_norm_w": kv_norm_w, "w_k": w_k, "w_v": w_v, "rel_bias": rel_bias,
        "b_w_q": b_w_q, "b_lambda_q1": b_lambda_q1, "b_lambda_k1": b_lambda_k1,
        "b_lambda_q2": b_lambda_q2, "b_lambda_k2": b_lambda_k2,
        "b_subln_w": b_subln_w, "b_out_proj": b_out_proj,
        "norm_pre_mix": norm_pre_mix, "norm_post_mix": norm_post_mix,
        "norm_pre_mlp": norm_pre_mlp, "norm_post_mlp": norm_post_mlp,
        "mlp_w1": mlp_w1, "mlp_w2": mlp_w2,
    }


def reference(x, a_in_proj, a_conv_w, a_conv_b, a_dt_bias, a_A_log, a_D, a_gate_norm_w, a_out_proj,
              kv_norm_w, w_k, w_v, rel_bias,
              b_w_q, b_lambda_q1, b_lambda_k1, b_lambda_q2, b_lambda_k2, b_subln_w, b_out_proj,
              norm_pre_mix, norm_post_mix, norm_pre_mlp, norm_post_mlp, mlp_w1, mlp_w2):
    bsz, S, _ = x.shape
    h = x
    k_shared = None
    v_shared = None
    for l in range(DEPTH):
        u = rms_norm(h, norm_pre_mix[l])
        if l < N_A:
            mix = mamba2_mixer(u, a_in_proj[l], a_conv_w[l], a_conv_b[l], a_dt_bias[l], a_A_log[l],
                               a_D[l], a_gate_norm_w[l], a_out_proj[l])
        else:
            if l == N_A:
                kv_in = rms_norm(h, kv_norm_w)
                k_shared = (kv_in @ w_k).reshape(bsz, S, N_DIFF_HEADS, 2, DIFF_HEAD_DIM)
                v_shared = (kv_in @ w_v).reshape(bsz, S, N_DIFF_HEADS, 2 * DIFF_HEAD_DIM)
            j = l - N_A
            lambda_init = 0.8 - 0.6 * math.exp(-0.3 * l)
            mix = diff_attention(u, k_shared, v_shared, b_w_q[j], b_lambda_q1[j], b_lambda_k1[j],
                                 b_lambda_q2[j], b_lambda_k2[j], b_subln_w[j], b_out_proj[j],
                                 rel_bias, lambda_init)
        h = h + rms_norm(mix, norm_post_mix[l])
        m = sq_relu_mlp(rms_norm(h, norm_pre_mlp[l]), mlp_w1[l], mlp_w2[l])
        h = h + rms_norm(m, norm_post_mlp[l])
    return h
```

```python
import functools
import math

import numpy as np
import jax
import jax.numpy as jnp
from jax import lax
from jax.experimental import pallas as pl
from jax.experimental.pallas import tpu as pltpu

SSM_HEAD_DIM = 64
D_STATE = 128
N_GROUPS = 8
CONV_W = 4
CHUNK = 128
DIFF_HEAD_DIM = 64
NUM_BUCKETS = 32
MAX_DISTANCE = 128
NORM_EPS = 1e-6
GATE_NORM_EPS = 1e-5
SUBLN_EPS = 1e-5

LANES = 128
CONV_TAIL_ROWS = 8
ATTN_BLOCK = 256
VMEM_LIMIT_BYTES = 56 * 1024 * 1024

BF16 = jnp.bfloat16
F32 = jnp.float32


def _params(*sem):
    return pltpu.CompilerParams(dimension_semantics=sem, vmem_limit_bytes=VMEM_LIMIT_BYTES)


def _rms_scale(xf, eps):
    return xf * lax.rsqrt(jnp.mean(xf * xf, axis=-1, keepdims=True) + eps)


def _silu(v):
    return v * (1.0 / (1.0 + jnp.exp(-v)))


def _dot(a, b):
    return jnp.dot(a, b, preferred_element_type=F32)


def _in_proj_kernel(x_ref, nw_ref, w_ref, wdt_ref, dtb_ref, zx_ref, dt_ref, u_ref):
    j = pl.program_id(1)

    @pl.when(j == 0)
    def _():
        xf = x_ref[...]
        u = (_rms_scale(xf, NORM_EPS) * nw_ref[...]).astype(BF16)
        u_ref[...] = u
        raw = _dot(u, wdt_ref[...]) + dtb_ref[...]
        dt_ref[...] = jnp.maximum(raw, 0.0) + jnp.log1p(jnp.exp(-jnp.abs(raw)))

    zx_ref[...] = _dot(u_ref[...], w_ref[...]).astype(zx_ref.dtype)


def _in_proj(x2d, norm_w, w_zx, w_dt, dt_bias, *, tm, tn):
    m, d = x2d.shape
    n = w_zx.shape[1]
    return pl.pallas_call(
        _in_proj_kernel,
        grid=(m // tm, n // tn),
        in_specs=[
            pl.BlockSpec((tm, d), lambda i, j: (i, 0)),
            pl.BlockSpec((1, d), lambda i, j: (0, 0)),
            pl.BlockSpec((d, tn), lambda i, j: (0, j)),
            pl.BlockSpec((d, LANES), lambda i, j: (0, 0)),
            pl.BlockSpec((1, LANES), lambda i, j: (0, 0)),
        ],
        out_specs=[
            pl.BlockSpec((tm, tn), lambda i, j: (i, j)),
            pl.BlockSpec((tm, LANES), lambda i, j: (i, 0)),
        ],
        out_shape=[
            jax.ShapeDtypeStruct((m, n), F32),
            jax.ShapeDtypeStruct((m, LANES), F32),
        ],
        scratch_shapes=[pltpu.VMEM((tm, d), BF16)],
        compiler_params=_params("arbitrary", "arbitrary"),
        name="in_proj",
    )(x2d, norm_w, w_zx, w_dt, dt_bias)


def _split3(v):
    hi = v.astype(BF16)
    r1 = v - hi.astype(F32)
    mid = r1.astype(BF16)
    lo = (r1 - mid.astype(F32)).astype(BF16)
    return hi, mid, lo


def _ssd_kernel(z_ref, x_ref, bc_ref, dt_ref, cwx_ref, cbx_ref, cwbc_ref, cbbc_ref,
                alog_ref, dskip_ref, gnw_ref, g_ref,
                state_ref, xext_ref, bcext_ref, cumT_ref, dtT_ref, wT_ref, *, heads_per_group):
    c = pl.program_id(1)
    q = CHUNK
    gp = heads_per_group * SSM_HEAD_DIM
    n_bc = N_GROUPS * D_STATE

    @pl.when(c == 0)
    def _():
        state_ref[...] = jnp.zeros_like(state_ref)
        xext_ref[0:CONV_TAIL_ROWS, :] = jnp.zeros((CONV_TAIL_ROWS, xext_ref.shape[1]), F32)
        bcext_ref[0:CONV_TAIL_ROWS, :] = jnp.zeros((CONV_TAIL_ROWS, bcext_ref.shape[1]), F32)

    xext_ref[CONV_TAIL_ROWS:, :] = x_ref[...]
    bcext_ref[CONV_TAIL_ROWS:, :] = bc_ref[...]

    dt = dt_ref[...]
    a = dt * (-jnp.exp(alog_ref[...]))
    row = lax.broadcasted_iota(jnp.int32, (q, q), 0)
    col = lax.broadcasted_iota(jnp.int32, (q, q), 1)
    tril = (col <= row).astype(BF16)
    a_hi, a_mid, a_lo = _split3(a)
    cum = _dot(tril, a_hi) + _dot(tril, a_mid) + _dot(tril, a_lo)
    cumT = cum.T
    dtT = dt.T
    cum_last = cumT[:, q - 1:q]
    cumT_ref[...] = cumT
    dtT_ref[...] = dtT
    wT_ref[...] = jnp.exp(cum_last - cumT) * dtT

    causal_T = row <= col

    def conv_silu(ext_ref, w_ref, b_ref, start, width):
        ext = ext_ref[:, pl.ds(start, width)]
        w = w_ref[:, pl.ds(start, width)]
        acc = b_ref[:, pl.ds(start, width)] + w[CONV_W - 1:CONV_W, :] * ext[CONV_TAIL_ROWS:, :]
        for s in range(1, CONV_W):
            shifted = pltpu.roll(ext, s, axis=0)[CONV_TAIL_ROWS:, :]
            acc = acc + w[CONV_W - 1 - s:CONV_W - s, :] * shifted
        return _silu(acc)

    def group_body(g, carry):
        xs = conv_silu(xext_ref, cwx_ref, cbx_ref, pl.multiple_of(g * gp, gp), gp)
        bg = conv_silu(bcext_ref, cwbc_ref, cbbc_ref, pl.multiple_of(g * D_STATE, D_STATE), D_STATE)
        cg = conv_silu(bcext_ref, cwbc_ref, cbbc_ref,
                       pl.multiple_of(n_bc + g * D_STATE, D_STATE), D_STATE)
        bg16 = bg.astype(BF16)
        cgT16 = cg.T.astype(BF16)
        cbT = _dot(bg16, cgT16)
        xT = xs.T
        h0 = pl.multiple_of(g * heads_per_group, heads_per_group)
        cum_g = cumT_ref[pl.ds(h0, heads_per_group), :]
        dt_g = dtT_ref[pl.ds(h0, heads_per_group), :]
        w_g = wT_ref[pl.ds(h0, heads_per_group), :]
        st = state_ref[g]
        inter = _dot(st.astype(BF16), cgT16)
        y_parts = []
        xw_parts = []
        dec_parts = []
        for r in range(heads_per_group):
            lo, hi = r * SSM_HEAD_DIM, (r + 1) * SSM_HEAD_DIM
            cum_row = cum_g[r:r + 1, :]
            cum_col = jnp.broadcast_to(cum_row, (q, q)).T
            seg = jnp.where(causal_T, cum_row - cum_col, -jnp.inf)
            mT = (cbT * jnp.exp(seg)).astype(BF16)
            xh = xT[lo:hi, :]
            xdt = (xh * dt_g[r:r + 1, :]).astype(BF16)
            y = _dot(xdt, mT)
            y = y + inter[lo:hi, :] * jnp.exp(cum_row)
            y = y + dskip_ref[h0 + r] * xh
            y_parts.append(y)
            xw_parts.append((xh * w_g[r:r + 1, :]).astype(BF16))
            dec = jnp.exp(cum_row[:, q - 1:q])
            dec_parts.append(jnp.broadcast_to(dec, (SSM_HEAD_DIM, D_STATE)))
        yT = jnp.concatenate(y_parts, axis=0)
        xw = jnp.concatenate(xw_parts, axis=0)
        decay = jnp.concatenate(dec_parts, axis=0)
        state_ref[g] = st * decay + _dot(xw, bg16)
        y_tm = yT.T
        z = z_ref[:, pl.ds(pl.multiple_of(g * gp, gp), gp)]
        gated = y_tm * _silu(z)
        gated = _rms_scale(gated, GATE_NORM_EPS) * gnw_ref[:, pl.ds(pl.multiple_of(g * gp, gp), gp)]
        g_ref[:, pl.ds(pl.multiple_of(g * gp, gp), gp)] = gated.astype(g_ref.dtype)
        return carry

    lax.fori_loop(0, N_GROUPS, group_body, 0)

    xext_ref[0:CONV_TAIL_ROWS, :] = x_ref[q - CONV_TAIL_ROWS:, :]
    bcext_ref[0:CONV_TAIL_ROWS, :] = bc_ref[q - CONV_TAIL_ROWS:, :]


def _ssd(zx, dt, conv_w, conv_b, a_log, d_skip, gate_norm_w, *, bsz, seq, d_inner):
    n_heads = d_inner // SSM_HEAD_DIM
    hpg = n_heads // N_GROUPS
    n_bc = N_GROUPS * D_STATE
    nc = seq // CHUNK
    zx3 = zx.reshape(bsz, seq, zx.shape[-1])
    dt3 = dt.reshape(bsz, seq, LANES)
    blk_x = d_inner // (2 * n_bc)
    assert d_inner % (2 * n_bc) == 0
    cw_x, cw_bc = conv_w[:, :d_inner], conv_w[:, d_inner:]
    cb_x, cb_bc = conv_b[None, :d_inner], conv_b[None, d_inner:]
    alog = jnp.pad(a_log, (0, LANES - n_heads))[None, :]
    kernel = functools.partial(_ssd_kernel, heads_per_group=hpg)
    const2 = lambda b, c: (0, 0)
    return pl.pallas_call(
        kernel,
        grid=(bsz, nc),
        in_specs=[
            pl.BlockSpec((None, CHUNK, d_inner), lambda b, c: (b, c, 0)),
            pl.BlockSpec((None, CHUNK, d_inner), lambda b, c: (b, c, 1)),
            pl.BlockSpec((None, CHUNK, 2 * n_bc), lambda b, c: (b, c, 2 * blk_x)),
            pl.BlockSpec((None, CHUNK, LANES), lambda b, c: (b, c, 0)),
            pl.BlockSpec((CONV_W, d_inner), const2),
            pl.BlockSpec((1, d_inner), const2),
            pl.BlockSpec((CONV_W, 2 * n_bc), const2),
            pl.BlockSpec((1, 2 * n_bc), const2),
            pl.BlockSpec((1, LANES), const2),
            pl.BlockSpec(memory_space=pltpu.SMEM),
            pl.BlockSpec((1, d_inner), const2),
        ],
        out_specs=pl.BlockSpec((None, CHUNK, d_inner), lambda b, c: (b, c, 0)),
        out_shape=jax.ShapeDtypeStruct((bsz, seq, d_inner), BF16),
        scratch_shapes=[
            pltpu.VMEM((N_GROUPS, hpg * SSM_HEAD_DIM, D_STATE), F32),
            pltpu.VMEM((CONV_TAIL_ROWS + CHUNK, d_inner), F32),
            pltpu.VMEM((CONV_TAIL_ROWS + CHUNK, 2 * n_bc), F32),
            pltpu.VMEM((LANES, CHUNK), F32),
            pltpu.VMEM((LANES, CHUNK), F32),
            pltpu.VMEM((LANES, CHUNK), F32),
        ],
        compiler_params=_params("arbitrary", "arbitrary"),
        name="ssd",
    )(zx3, zx3, zx3, dt3, cw_x, cb_x, cw_bc, cb_bc, alog, d_skip, gate_norm_w[None, :])


def _out_proj_kernel(a_ref, w_ref, res_ref, nw_ref, o_ref):
    k = pl.program_id(1)
    part = _dot(a_ref[...], w_ref[...])

    @pl.when(k == 0)
    def _():
        o_ref[...] = part

    @pl.when(k > 0)
    def _():
        o_ref[...] += part

    @pl.when(k == pl.num_programs(1) - 1)
    def _():
        o_ref[...] = res_ref[...] + _rms_scale(o_ref[...], NORM_EPS) * nw_ref[...]


def _out_proj(a, w, res, norm_w, *, tm, tk):
    m, kdim = a.shape
    d = w.shape[1]
    return pl.pallas_call(
        _out_proj_kernel,
        grid=(m // tm, kdim // tk),
        in_specs=[
            pl.BlockSpec((tm, tk), lambda i, k: (i, k)),
            pl.BlockSpec((tk, d), lambda i, k: (k, 0)),
            pl.BlockSpec((tm, d), lambda i, k: (i, 0)),
            pl.BlockSpec((1, d), lambda i, k: (0, 0)),
        ],
        out_specs=pl.BlockSpec((tm, d), lambda i, k: (i, 0)),
        out_shape=jax.ShapeDtypeStruct((m, d), F32),
        compiler_params=_params("arbitrary", "arbitrary"),
        name="out_proj",
    )(a, w, res, norm_w)


def _mlp_kernel(h_ref, pre_ref, w1_ref, w2_ref, post_ref, o_ref, u_ref):
    f = pl.program_id(1)

    @pl.when(f == 0)
    def _():
        u_ref[...] = (_rms_scale(h_ref[...], NORM_EPS) * pre_ref[...]).astype(BF16)

    a = jnp.maximum(_dot(u_ref[...], w1_ref[...]), 0.0)
    part = _dot((a * a).astype(BF16), w2_ref[...])

    @pl.when(f == 0)
    def _():
        o_ref[...] = part

    @pl.when(f > 0)
    def _():
        o_ref[...] += part

    @pl.when(f == pl.num_programs(1) - 1)
    def _():
        o_ref[...] = h_ref[...] + _rms_scale(o_ref[...], NORM_EPS) * post_ref[...]


def _mlp(h, pre_w, w1, w2, post_w, *, tm, tf):
    m, d = h.shape
    ff = w1.shape[1]
    return pl.pallas_call(
        _mlp_kernel,
        grid=(m // tm, ff // tf),
        in_specs=[
            pl.BlockSpec((tm, d), lambda i, f: (i, 0)),
            pl.BlockSpec((1, d), lambda i, f: (0, 0)),
            pl.BlockSpec((d, tf), lambda i, f: (0, f)),
            pl.BlockSpec((tf, d), lambda i, f: (f, 0)),
            pl.BlockSpec((1, d), lambda i, f: (0, 0)),
        ],
        out_specs=pl.BlockSpec((tm, d), lambda i, f: (i, 0)),
        out_shape=jax.ShapeDtypeStruct((m, d), F32),
        scratch_shapes=[pltpu.VMEM((tm, d), BF16)],
        compiler_params=_params("arbitrary", "arbitrary"),
        name="mlp",
    )(h, pre_w, w1, w2, post_w)


def _qkv_kernel(h_ref, qn_ref, kvn_ref, w_ref, o_ref, u_ref, *, n_q_tiles):
    j = pl.program_id(1)

    @pl.when(j == 0)
    def _():
        xhat = _rms_scale(h_ref[...], NORM_EPS)
        u_ref[0] = (xhat * qn_ref[...]).astype(BF16)
        u_ref[1] = (xhat * kvn_ref[...]).astype(BF16)

    sel = (j >= n_q_tiles).astype(jnp.int32)
    o_ref[...] = _dot(u_ref[sel], w_ref[...]).astype(o_ref.dtype)


def _qkv(h, q_norm_w, kv_norm_w, w_qkv, *, tm, tn):
    m, d = h.shape
    n = w_qkv.shape[1]
    kernel = functools.partial(_qkv_kernel, n_q_tiles=(n // 3) // tn)
    return pl.pallas_call(
        kernel,
        grid=(m // tm, n // tn),
        in_specs=[
            pl.BlockSpec((tm, d), lambda i, j: (i, 0)),
            pl.BlockSpec((1, d), lambda i, j: (0, 0)),
            pl.BlockSpec((1, d), lambda i, j: (0, 0)),
            pl.BlockSpec((d, tn), lambda i, j: (0, j)),
        ],
        out_specs=pl.BlockSpec((tm, tn), lambda i, j: (i, j)),
        out_shape=jax.ShapeDtypeStruct((m, n), BF16),
        scratch_shapes=[pltpu.VMEM((2, tm, d), BF16)],
        compiler_params=_params("arbitrary", "arbitrary"),
        name="qkv",
    )(h, q_norm_w, kv_norm_w, w_qkv)


def _t5_bucket_np(dist):
    n = np.maximum(dist, 0)
    max_exact = NUM_BUCKETS // 2
    nf = np.maximum(n, 1).astype(np.float32)
    large = max_exact + (np.log(nf / np.float32(max_exact)) / np.float32(math.log(MAX_DISTANCE / max_exact))
                         * np.float32(NUM_BUCKETS - max_exact)).astype(np.int32)
    large = np.minimum(large, NUM_BUCKETS - 1)
    return np.where(n < max_exact, n, large).astype(np.int32)


def _bucket_tiles(t):
    i = np.arange(t)[:, None]
    j = np.arange(t)[None, :]
    diag = np.where(i >= j, _t5_bucket_np(i - j), -1)
    left = _t5_bucket_np(t + i - j)
    return np.stack([diag, left]).astype(np.int32)


def _bias_kernel(idx_ref, rel_ref, o_ref):
    h = pl.program_id(0)
    idx = idx_ref[...]
    acc = jnp.where(idx < 0, -jnp.inf, 0.0).astype(F32)
    for b in range(NUM_BUCKETS):
        acc = jnp.where(idx == b, rel_ref[b, h], acc)
    o_ref[...] = acc


def _bias_tiles(rel_bias, n_heads, t):
    idx = jnp.asarray(_bucket_tiles(t))
    return pl.pallas_call(
        _bias_kernel,
        grid=(n_heads,),
        in_specs=[
            pl.BlockSpec((2, t, t), lambda h: (0, 0, 0)),
            pl.BlockSpec(memory_space=pltpu.SMEM),
        ],
        out_specs=pl.BlockSpec((None, 2, t, t), lambda h: (h, 0, 0, 0)),
        out_shape=jax.ShapeDtypeStruct((n_heads, 2, t, t), F32),
        compiler_params=_params("arbitrary"),
        name="bias_tiles",
    )(idx, rel_bias)


def _attn_kernel(q_ref, k_ref, v_ref, bias_ref, rel_ref, lq1_ref, lk1_ref, lq2_ref, lk2_ref, sw_ref,
                 o_ref, m_ref, l_ref, acc_ref, *, lambda_init):
    h = pl.program_id(1)
    i = pl.program_id(2)
    t = ATTN_BLOCK
    dh = DIFF_HEAD_DIM

    q = q_ref[...]
    lane = lax.broadcasted_iota(jnp.int32, q.shape, 1)
    zero = jnp.zeros_like(q)
    qq = jnp.concatenate([jnp.where(lane < dh, q, zero), jnp.where(lane >= dh, q, zero)], axis=0)

    m_ref[...] = jnp.full(m_ref.shape, -jnp.inf, F32)
    l_ref[...] = jnp.zeros(l_ref.shape, F32)
    acc_ref[...] = jnp.zeros(acc_ref.shape, F32)

    def update(j, bias):
        start = pl.multiple_of(j * t, t)
        kj = k_ref[pl.ds(start, t), :]
        vj = v_ref[pl.ds(start, t), :]
        s = lax.dot_general(qq, kj, (((1,), (1,)), ((), ())), preferred_element_type=F32)
        s = s + bias
        m_prev = m_ref[...]
        m_new = jnp.maximum(m_prev, jnp.max(s, axis=-1, keepdims=True))
        alpha = jnp.exp(m_prev - m_new)
        p = jnp.exp(s - m_new[:, 0:1])
        l_ref[...] = alpha * l_ref[...] + jnp.sum(p, axis=-1, keepdims=True)
        acc_ref[...] = alpha[:, 0:2 * dh] * acc_ref[...] + _dot(p.astype(BF16), vj)
        m_ref[...] = m_new

    diag = bias_ref[0]
    update(i, jnp.concatenate([diag, diag], axis=0))

    @pl.when(i > 0)
    def _():
        left = bias_ref[1]
        update(i - 1, jnp.concatenate([left, left], axis=0))

    far_bias = rel_ref[NUM_BUCKETS - 1, h]

    def far_body(j, carry):
        update(j, far_bias)
        return carry

    lax.fori_loop(0, jnp.maximum(i - 1, 0), far_body, 0)

    lam = (jnp.exp(jnp.sum(lq1_ref[...] * lk1_ref[...], axis=-1, keepdims=True))
           - jnp.exp(jnp.sum(lq2_ref[...] * lk2_ref[...], axis=-1, keepdims=True))
           + lambda_init)
    out = acc_ref[...] / l_ref[:, 0:2 * dh]
    o = out[0:t, :] - lam * out[t:, :]
    o = _rms_scale(o, SUBLN_EPS) * sw_ref[...] * (1.0 - lambda_init)
    o_ref[...] = o.astype(o_ref.dtype)


def _attention(qkv, bias_tiles, rel_bias, lq1, lk1, lq2, lk2, subln_w, *, bsz, seq, n_heads, lambda_init):
    t = ATTN_BLOCK
    hw = 2 * DIFF_HEAD_DIM
    qkv3 = qkv.reshape(bsz, seq, 3 * n_heads * hw)
    kernel = functools.partial(_attn_kernel, lambda_init=lambda_init)
    vec = lambda b, h, i: (0, 0)
    return pl.pallas_call(
        kernel,
        grid=(bsz, n_heads, seq // t),
        in_specs=[
            pl.BlockSpec((None, t, hw), lambda b, h, i: (b, i, h)),
            pl.BlockSpec((None, seq, hw), lambda b, h, i: (b, 0, n_heads + h)),
            pl.BlockSpec((None, seq, hw), lambda b, h, i: (b, 0, 2 * n_heads + h)),
            pl.BlockSpec((None, 2, t, t), lambda b, h, i: (h, 0, 0, 0)),
            pl.BlockSpec(memory_space=pltpu.SMEM),
            pl.BlockSpec((1, DIFF_HEAD_DIM), vec),
            pl.BlockSpec((1, DIFF_HEAD_DIM), vec),
            pl.BlockSpec((1, DIFF_HEAD_DIM), vec),
            pl.BlockSpec((1, DIFF_HEAD_DIM), vec),
            pl.BlockSpec((1, hw), vec),
        ],
        out_specs=pl.BlockSpec((None, t, hw), lambda b, h, i: (b, i, h)),
        out_shape=jax.ShapeDtypeStruct((bsz, seq, n_heads * hw), BF16),
        scratch_shapes=[
            pltpu.VMEM((2 * t, LANES), F32),
            pltpu.VMEM((2 * t, LANES), F32),
            pltpu.VMEM((2 * t, hw), F32),
        ],
        compiler_params=_params("arbitrary", "arbitrary", "arbitrary"),
        name="diff_attention",
    )(qkv3, qkv3, qkv3, bias_tiles, rel_bias, lq1[None, :], lk1[None, :], lq2[None, :], lk2[None, :],
      subln_w[None, :])


def kernel(x, a_in_proj, a_conv_w, a_conv_b, a_dt_bias, a_A_log, a_D, a_gate_norm_w, a_out_proj, kv_norm_w, w_k, w_v, rel_bias, b_w_q, b_lambda_q1, b_lambda_k1, b_lambda_q2, b_lambda_k2, b_subln_w, b_out_proj, norm_pre_mix, norm_post_mix, norm_pre_mlp, norm_post_mlp, mlp_w1, mlp_w2):
    bsz, seq, d = x.shape
    m = bsz * seq
    d_inner = a_out_proj.shape[1]
    n_ssm_heads = a_A_log.shape[1]
    n_zx = 2 * d_inner + 2 * N_GROUPS * D_STATE
    n_heads = d // (2 * DIFF_HEAD_DIM)
    x2d = x.reshape(m, d)

    tm = min(1024, m)
    tm_mlp = min(512, m)

    w_in = a_in_proj[0]
    w_zx = w_in[:, :n_zx].astype(BF16)
    w_dt = jnp.pad(w_in[:, n_zx:], ((0, 0), (0, LANES - n_ssm_heads))).astype(BF16)
    dt_bias = jnp.pad(a_dt_bias[0], (0, LANES - n_ssm_heads))[None, :]
    zx, dt = _in_proj(x2d, norm_pre_mix[0][None, :], w_zx, w_dt, dt_bias, tm=tm, tn=min(1024, n_zx))
    g = _ssd(zx, dt, a_conv_w[0], a_conv_b[0], a_A_log[0], a_D[0], a_gate_norm_w[0],
             bsz=bsz, seq=seq, d_inner=d_inner)
    h = _out_proj(g.reshape(m, d_inner), a_out_proj[0].astype(BF16), x2d, norm_post_mix[0][None, :],
                  tm=tm_mlp, tk=min(1024, d_inner))
    h = _mlp(h, norm_pre_mlp[0][None, :], mlp_w1[0].astype(BF16), mlp_w2[0].astype(BF16),
             norm_post_mlp[0][None, :], tm=tm_mlp, tf=min(1024, mlp_w1.shape[2]))

    scale = DIFF_HEAD_DIM ** -0.5
    w_qkv = jnp.concatenate([b_w_q[0] * scale, w_k, w_v], axis=1).astype(BF16)
    qkv = _qkv(h, norm_pre_mix[1][None, :], kv_norm_w[None, :], w_qkv, tm=tm, tn=min(1024, d))
    lambda_init = 0.8 - 0.6 * math.exp(-0.3 * 1)
    bias = _bias_tiles(rel_bias, n_heads, ATTN_BLOCK)
    o = _attention(qkv, bias, rel_bias, b_lambda_q1[0], b_lambda_k1[0], b_lambda_q2[0], b_lambda_k2[0],
                   b_subln_w[0], bsz=bsz, seq=seq, n_heads=n_heads, lambda_init=lambda_init)
    h = _out_proj(o.reshape(m, d), b_out_proj[0].astype(BF16), h, norm_post_mix[1][None, :],
                  tm=tm_mlp, tk=min(1024, d))
    h = _mlp(h, norm_pre_mlp[1][None, :], mlp_w1[1].astype(BF16), mlp_w2[1].astype(BF16),
             norm_post_mlp[1][None, :], tm=tm_mlp, tf=min(1024, mlp_w1.shape[2]))
    return h.reshape(bsz, seq, d)
```

```python
import functools
import math

import numpy as np
import jax
import jax.numpy as jnp
from jax import lax
from jax.experimental import pallas as pl
from jax.experimental.pallas import tpu as pltpu

SSM_HEAD_DIM = 64
D_STATE = 128
N_GROUPS = 8
CONV_W = 4
CHUNK = 128
DIFF_HEAD_DIM = 64
NUM_BUCKETS = 32
MAX_DISTANCE = 128
NORM_EPS = 1e-6
GATE_NORM_EPS = 1e-5
SUBLN_EPS = 1e-5

LANES = 128
CONV_TAIL_ROWS = 8
ATTN_BLOCK = 256
VMEM_LIMIT_BYTES = 56 * 1024 * 1024

BF16 = jnp.bfloat16
F32 = jnp.float32


def _params(*sem):
    return pltpu.CompilerParams(dimension_semantics=sem, vmem_limit_bytes=VMEM_LIMIT_BYTES)


def _rms_scale(xf, eps):
    return xf * lax.rsqrt(jnp.mean(xf * xf, axis=-1, keepdims=True) + eps)


def _silu(v):
    return v * (1.0 / (1.0 + jnp.exp(-v)))


def _dot(a, b):
    return jnp.dot(a, b, preferred_element_type=F32)


def _in_proj_kernel(x_ref, nw_ref, w_ref, wdt_ref, dtb_ref, zx_ref, dt_ref, u_ref):
    j = pl.program_id(1)

    @pl.when(j == 0)
    def _():
        xf = x_ref[...]
        u = (_rms_scale(xf, NORM_EPS) * nw_ref[...]).astype(BF16)
        u_ref[...] = u
        raw = _dot(u, wdt_ref[...]) + dtb_ref[...]
        dt_ref[...] = jnp.maximum(raw, 0.0) + jnp.log1p(jnp.exp(-jnp.abs(raw)))

    zx_ref[...] = _dot(u_ref[...], w_ref[...]).astype(zx_ref.dtype)


def _in_proj(x2d, norm_w, w_zx, w_dt, dt_bias, *, tm, tn):
    m, d = x2d.shape
    n = w_zx.shape[1]
    return pl.pallas_call(
        _in_proj_kernel,
        grid=(m // tm, n // tn),
        in_specs=[
            pl.BlockSpec((tm, d), lambda i, j: (i, 0)),
            pl.BlockSpec((1, d), lambda i, j: (0, 0)),
            pl.BlockSpec((d, tn), lambda i, j: (0, j)),
            pl.BlockSpec((d, LANES), lambda i, j: (0, 0)),
            pl.BlockSpec((1, LANES), lambda i, j: (0, 0)),
        ],
        out_specs=[
            pl.BlockSpec((tm, tn), lambda i, j: (i, j)),
            pl.BlockSpec((tm, LANES), lambda i, j: (i, 0)),
        ],
        out_shape=[
            jax.ShapeDtypeStruct((m, n), F32),
            jax.ShapeDtypeStruct((m, LANES), F32),
        ],
        scratch_shapes=[pltpu.VMEM((tm, d), BF16)],
        compiler_params=_params("arbitrary", "arbitrary"),
        name="in_proj",
    )(x2d, norm_w, w_zx, w_dt, dt_bias)


def _split3(v):
    hi = v.astype(BF16)
    r1 = v - hi.astype(F32)
    mid = r1.astype(BF16)
    lo = (r1 - mid.astype(F32)).astype(BF16)
    return hi, mid, lo


def _ssd_kernel(z_ref, x_ref, bc_ref, dt_ref, cwx_ref, cbx_ref, cwbc_ref, cbbc_ref,
                alog_ref, dskip_ref, gnw_ref, g_ref,
                state_ref, xext_ref, bcext_ref, cumT_ref, dtT_ref, wT_ref, *, heads_per_group):
    c = pl.program_id(1)
    q = CHUNK
    gp = heads_per_group * SSM_HEAD_DIM
    n_bc = N_GROUPS * D_STATE

    @pl.when(c == 0)
    def _():
        state_ref[...] = jnp.zeros_like(state_ref)
        xext_ref[0:CONV_TAIL_ROWS, :] = jnp.zeros((CONV_TAIL_ROWS, xext_ref.shape[1]), F32)
        bcext_ref[0:CONV_TAIL_ROWS, :] = jnp.zeros((CONV_TAIL_ROWS, bcext_ref.shape[1]), F32)

    xext_ref[CONV_TAIL_ROWS:, :] = x_ref[...]
    bcext_ref[CONV_TAIL_ROWS:, :] = bc_ref[...]

    dt = dt_ref[...]
    a = dt * (-jnp.exp(alog_ref[...]))
    row = lax.broadcasted_iota(jnp.int32, (q, q), 0)
    col = lax.broadcasted_iota(jnp.int32, (q, q), 1)
    tril = (col <= row).astype(BF16)
    a_hi, a_mid, a_lo = _split3(a)
    cum = _dot(tril, a_hi) + _dot(tril, a_mid) + _dot(tril, a_lo)
    cumT = cum.T
    dtT = dt.T
    cum_last = cumT[:, q - 1:q]
    cumT_ref[...] = cumT
    dtT_ref[...] = dtT
    wT_ref[...] = jnp.exp(cum_last - cumT) * dtT

    causal_T = row <= col

    def conv_silu(ext_ref, w_ref, b_ref, start, width):
        ext = ext_ref[:, pl.ds(start, width)]
        w = w_ref[:, pl.ds(start, width)]
        acc = b_ref[:, pl.ds(start, width)] + w[CONV_W - 1:CONV_W, :] * ext[CONV_TAIL_ROWS:, :]
        for s in range(1, CONV_W):
            shifted = pltpu.roll(ext, s, axis=0)[CONV_TAIL_ROWS:, :]
            acc = acc + w[CONV_W - 1 - s:CONV_W - s, :] * shifted
        return _silu(acc)

    def group_body(g, carry):
        xs = conv_silu(xext_ref, cwx_ref, cbx_ref, pl.multiple_of(g * gp, gp), gp)
        bg = conv_silu(bcext_ref, cwbc_ref, cbbc_ref, pl.multiple_of(g * D_STATE, D_STATE), D_STATE)
        cg = conv_silu(bcext_ref, cwbc_ref, cbbc_ref,
                       pl.multiple_of(n_bc + g * D_STATE, D_STATE), D_STATE)
        bg16 = bg.astype(BF16)
        cgT16 = cg.T.astype(BF16)
        cbT = _dot(bg16, cgT16)
        xT = xs.T
        h0 = pl.multiple_of(g * heads_per_group, heads_per_group)
        cum_g = cumT_ref[pl.ds(h0, heads_per_group), :]
        dt_g = dtT_ref[pl.ds(h0, heads_per_group), :]
        w_g = wT_ref[pl.ds(h0, heads_per_group), :]
        st = state_ref[g]
        inter = _dot(st.astype(BF16), cgT16)
        y_parts = []
        xw_parts = []
        dec_parts = []
        for r in range(heads_per_group):
            lo, hi = r * SSM_HEAD_DIM, (r + 1) * SSM_HEAD_DIM
            cum_row = cum_g[r:r + 1, :]
            cum_col = jnp.broadcast_to(cum_row, (q, q)).T
            seg = jnp.where(causal_T, cum_row - cum_col, -jnp.inf)
            mT = (cbT * jnp.exp(seg)).astype(BF16)
            xh = xT[lo:hi, :]
            xdt = (xh * dt_g[r:r + 1, :]).astype(BF16)
            y = _dot(xdt, mT)
            y = y + inter[lo:hi, :] * jnp.exp(cum_row)
            y = y + dskip_ref[h0 + r] * xh
            y_parts.append(y)
            xw_parts.append((xh * w_g[r:r + 1, :]).astype(BF16))
            dec = jnp.exp(cum_row[:, q - 1:q])
            dec_parts.append(jnp.broadcast_to(dec, (SSM_HEAD_DIM, D_STATE)))
        yT = jnp.concatenate(y_parts, axis=0)
        xw = jnp.concatenate(xw_parts, axis=0)
        decay = jnp.concatenate(dec_parts, axis=0)
        state_ref[g] = st * decay + _dot(xw, bg16)
        y_tm = yT.T
        z = z_ref[:, pl.ds(pl.multiple_of(g * gp, gp), gp)]
        gated = y_tm * _silu(z)
        gated = _rms_scale(gated, GATE_NORM_EPS) * gnw_ref[:, pl.ds(pl.multiple_of(g * gp, gp), gp)]
        g_ref[:, pl.ds(pl.multiple_of(g * gp, gp), gp)] = gated.astype(g_ref.dtype)
        return carry

    lax.fori_loop(0, N_GROUPS, group_body, 0)

    xext_ref[0:CONV_TAIL_ROWS, :] = x_ref[q - CONV_TAIL_ROWS:, :]
    bcext_ref[0:CONV_TAIL_ROWS, :] = bc_ref[q - CONV_TAIL_ROWS:, :]


def _ssd(zx, dt, conv_w, conv_b, a_log, d_skip, gate_norm_w, *, bsz, seq, d_inner):
    n_heads = d_inner // SSM_HEAD_DIM
    hpg = n_heads // N_GROUPS
    n_bc = N_GROUPS * D_STATE
    nc = seq // CHUNK
    zx3 = zx.reshape(bsz, seq, zx.shape[-1])
    dt3 = dt.reshape(bsz, seq, LANES)
    blk_x = d_inner // (2 * n_bc)
    assert d_inner % (2 * n_bc) == 0
    cw_x, cw_bc = conv_w[:, :d_inner], conv_w[:, d_inner:]
    cb_x, cb_bc = conv_b[None, :d_inner], conv_b[None, d_inner:]
    alog = jnp.pad(a_log, (0, LANES - n_heads))[None, :]
    kernel = functools.partial(_ssd_kernel, heads_per_group=hpg)
    const2 = lambda b, c: (0, 0)
    return pl.pallas_call(
        kernel,
        grid=(bsz, nc),
        in_specs=[
            pl.BlockSpec((None, CHUNK, d_inner), lambda b, c: (b, c, 0)),
            pl.BlockSpec((None, CHUNK, d_inner), lambda b, c: (b, c, 1)),
            pl.BlockSpec((None, CHUNK, 2 * n_bc), lambda b, c: (b, c, 2 * blk_x)),
            pl.BlockSpec((None, CHUNK, LANES), lambda b, c: (b, c, 0)),
            pl.BlockSpec((CONV_W, d_inner), const2),
            pl.BlockSpec((1, d_inner), const2),
            pl.BlockSpec((CONV_W, 2 * n_bc), const2),
            pl.BlockSpec((1, 2 * n_bc), const2),
            pl.BlockSpec((1, LANES), const2),
            pl.BlockSpec(memory_space=pltpu.SMEM),
            pl.BlockSpec((1, d_inner), const2),
        ],
        out_specs=pl.BlockSpec((None, CHUNK, d_inner), lambda b, c: (b, c, 0)),
        out_shape=jax.ShapeDtypeStruct((bsz, seq, d_inner), BF16),
        scratch_shapes=[
            pltpu.VMEM((N_GROUPS, hpg * SSM_HEAD_DIM, D_STATE), F32),
            pltpu.VMEM((CONV_TAIL_ROWS + CHUNK, d_inner), F32),
            pltpu.VMEM((CONV_TAIL_ROWS + CHUNK, 2 * n_bc), F32),
            pltpu.VMEM((LANES, CHUNK), F32),
            pltpu.VMEM((LANES, CHUNK), F32),
            pltpu.VMEM((LANES, CHUNK), F32),
        ],
        compiler_params=_params("arbitrary", "arbitrary"),
        name="ssd",
    )(zx3, zx3, zx3, dt3, cw_x, cb_x, cw_bc, cb_bc, alog, d_skip, gate_norm_w[None, :])


def _out_proj_kernel(a_ref, w_ref, res_ref, nw_ref, o_ref):
    k = pl.program_id(1)
    part = _dot(a_ref[...], w_ref[...])

    @pl.when(k == 0)
    def _():
        o_ref[...] = part

    @pl.when(k > 0)
    def _():
        o_ref[...] += part

    @pl.when(k == pl.num_programs(1) - 1)
    def _():
        o_ref[...] = res_ref[...] + _rms_scale(o_ref[...], NORM_EPS) * nw_ref[...]


def _out_proj(a, w, res, norm_w, *, tm, tk):
    m, kdim = a.shape
    d = w.shape[1]
    return pl.pallas_call(
        _out_proj_kernel,
        grid=(m // tm, kdim // tk),
        in_specs=[
            pl.BlockSpec((tm, tk), lambda i, k: (i, k)),
            pl.BlockSpec((tk, d), lambda i, k: (k, 0)),
            pl.BlockSpec((tm, d), lambda i, k: (i, 0)),
            pl.BlockSpec((1, d), lambda i, k: (0, 0)),
        ],
        out_specs=pl.BlockSpec((tm, d), lambda i, k: (i, 0)),
        out_shape=jax.ShapeDtypeStruct((m, d), F32),
        compiler_params=_params("arbitrary", "arbitrary"),
        name="out_proj",
    )(a, w, res, norm_w)


def _mlp_kernel(h_ref, pre_ref, w1_ref, w2_ref, post_ref, o_ref, u_ref):
    f = pl.program_id(1)

    @pl.when(f == 0)
    def _():
        u_ref[...] = (_rms_scale(h_ref[...], NORM_EPS) * pre_ref[...]).astype(BF16)

    a = jnp.maximum(_dot(u_ref[...], w1_ref[...]), 0.0)
    part = _dot((a * a).astype(BF16), w2_ref[...])

    @pl.when(f == 0)
    def _():
        o_ref[...] = part

    @pl.when(f > 0)
    def _():
        o_ref[...] += part

    @pl.when(f == pl.num_programs(1) - 1)
    def _():
        o_ref[...] = h_ref[...] + _rms_scale(o_ref[...], NORM_EPS) * post_ref[...]


def _mlp(h, pre_w, w1, w2, post_w, *, tm, tf):
    m, d = h.shape
    ff = w1.shape[1]
    return pl.pallas_call(
        _mlp_kernel,
        grid=(m // tm, ff // tf),
        in_specs=[
            pl.BlockSpec((tm, d), lambda i, f: (i, 0)),
            pl.BlockSpec((1, d), lambda i, f: (0, 0)),
            pl.BlockSpec((d, tf), lambda i, f: (0, f)),
            pl.BlockSpec((tf, d), lambda i, f: (f, 0)),
            pl.BlockSpec((1, d), lambda i, f: (0, 0)),
        ],
        out_specs=pl.BlockSpec((tm, d), lambda i, f: (i, 0)),
        out_shape=jax.ShapeDtypeStruct((m, d), F32),
        scratch_shapes=[pltpu.VMEM((tm, d), BF16)],
        compiler_params=_params("arbitrary", "arbitrary"),
        name="mlp",
    )(h, pre_w, w1, w2, post_w)


def _qkv_kernel(h_ref, qn_ref, kvn_ref, w_ref, o_ref, u_ref, *, n_q_tiles):
    j = pl.program_id(1)

    @pl.when(j == 0)
    def _():
        xhat = _rms_scale(h_ref[...], NORM_EPS)
        u_ref[0] = (xhat * qn_ref[...]).astype(BF16)
        u_ref[1] = (xhat * kvn_ref[...]).astype(BF16)

    sel = (j >= n_q_tiles).astype(jnp.int32)
    o_ref[...] = _dot(u_ref[sel], w_ref[...]).astype(o_ref.dtype)


def _qkv(h, q_norm_w, kv_norm_w, w_qkv, *, tm, tn):
    m, d = h.shape
    n = w_qkv.shape[1]
    kernel = functools.partial(_qkv_kernel, n_q_tiles=(n // 3) // tn)
    return pl.pallas_call(
        kernel,
        grid=(m // tm, n // tn),
        in_specs=[
            pl.BlockSpec((tm, d), lambda i, j: (i, 0)),
            pl.BlockSpec((1, d), lambda i, j: (0, 0)),
            pl.BlockSpec((1, d), lambda i, j: (0, 0)),
            pl.BlockSpec((d, tn), lambda i, j: (0, j)),
        ],
        out_specs=pl.BlockSpec((tm, tn), lambda i, j: (i, j)),
        out_shape=jax.ShapeDtypeStruct((m, n), BF16),
        scratch_shapes=[pltpu.VMEM((2, tm, d), BF16)],
        compiler_params=_params("arbitrary", "arbitrary"),
        name="qkv",
    )(h, q_norm_w, kv_norm_w, w_qkv)


def _t5_bucket_np(dist):
    n = np.maximum(dist, 0)
    max_exact = NUM_BUCKETS // 2
    nf = np.maximum(n, 1).astype(np.float32)
    large = max_exact + (np.log(nf / np.float32(max_exact)) / np.float32(math.log(MAX_DISTANCE / max_exact))
                         * np.float32(NUM_BUCKETS - max_exact)).astype(np.int32)
    large = np.minimum(large, NUM_BUCKETS - 1)
    return np.where(n < max_exact, n, large).astype(np.int32)


def _bucket_tiles(t):
    i = np.arange(t)[:, None]
    j = np.arange(t)[None, :]
    diag = np.where(i >= j, _t5_bucket_np(i - j), -1)
    left = _t5_bucket_np(t + i - j)
    return np.stack([diag, left]).astype(np.int32)


def _bias_kernel(idx_ref, rel_ref, o_ref):
    h = pl.program_id(0)
    idx = idx_ref[...]
    acc = jnp.where(idx < 0, -jnp.inf, 0.0).astype(F32)
    for b in range(NUM_BUCKETS):
        acc = jnp.where(idx == b, rel_ref[b, h], acc)
    o_ref[...] = acc


def _bias_tiles(rel_bias, n_heads, t):
    idx = jnp.asarray(_bucket_tiles(t))
    return pl.pallas_call(
        _bias_kernel,
        grid=(n_heads,),
        in_specs=[
            pl.BlockSpec((2, t, t), lambda h: (0, 0, 0)),
            pl.BlockSpec(memory_space=pltpu.SMEM),
        ],
        out_specs=pl.BlockSpec((None, 2, t, t), lambda h: (h, 0, 0, 0)),
        out_shape=jax.ShapeDtypeStruct((n_heads, 2, t, t), F32),
        compiler_params=_params("arbitrary"),
        name="bias_tiles",
    )(idx, rel_bias)


def _attn_kernel(q_ref, k_ref, v_ref, bias_ref, rel_ref, lq1_ref, lk1_ref, lq2_ref, lk2_ref, sw_ref,
                 o_ref, vext_ref, *, lambda_init):
    h = pl.program_id(1)
    t = ATTN_BLOCK
    dh = DIFF_HEAD_DIM
    hw = 2 * dh
    seq = q_ref.shape[0]

    vext_ref[:, 0:hw] = v_ref[...]
    vext_ref[:, hw:] = jnp.ones((seq, hw), BF16)

    lam = (jnp.exp(jnp.sum(lq1_ref[...] * lk1_ref[...], axis=-1, keepdims=True))
           - jnp.exp(jnp.sum(lq2_ref[...] * lk2_ref[...], axis=-1, keepdims=True))
           + lambda_init)
    far_bias = rel_ref[NUM_BUCKETS - 1, h]
    diag = bias_ref[0]
    diag2 = jnp.concatenate([diag, diag], axis=0)
    left = bias_ref[1]
    left2 = jnp.concatenate([left, left], axis=0)

    for i in range(seq // t):
        n_kv = (i + 1) * t
        q = q_ref[i * t:(i + 1) * t, :]
        lane = lax.broadcasted_iota(jnp.int32, q.shape, 1)
        zero = jnp.zeros_like(q)
        qq = jnp.concatenate([jnp.where(lane < dh, q, zero), jnp.where(lane >= dh, q, zero)], axis=0)
        s = lax.dot_general(qq, k_ref[0:n_kv, :], (((1,), (1,)), ((), ())),
                            preferred_element_type=F32)
        s_diag = s[:, n_kv - t:] + diag2
        m = jnp.max(s_diag, axis=-1, keepdims=True)
        parts = []
        if i >= 1:
            s_left = s[:, n_kv - 2 * t:n_kv - t] + left2
            m = jnp.maximum(m, jnp.max(s_left, axis=-1, keepdims=True))
        if i >= 2:
            s_far = s[:, 0:n_kv - 2 * t]
            m = jnp.maximum(m, jnp.max(s_far, axis=-1, keepdims=True) + far_bias)
            parts.append(jnp.exp(s_far - (m - far_bias)).astype(BF16))
        if i >= 1:
            parts.append(jnp.exp(s_left - m).astype(BF16))
        parts.append(jnp.exp(s_diag - m).astype(BF16))
        p = parts[0] if len(parts) == 1 else jnp.concatenate(parts, axis=1)
        acc = _dot(p, vext_ref[0:n_kv, :])
        out = acc[:, 0:hw] / acc[:, hw:]
        o = out[0:t, :] - lam * out[t:, :]
        o = _rms_scale(o, SUBLN_EPS) * sw_ref[...] * (1.0 - lambda_init)
        o_ref[i * t:(i + 1) * t, :] = o.astype(o_ref.dtype)


def _attention(qkv, bias_tiles, rel_bias, lq1, lk1, lq2, lk2, subln_w, *, bsz, seq, n_heads, lambda_init):
    t = ATTN_BLOCK
    hw = 2 * DIFF_HEAD_DIM
    qkv3 = qkv.reshape(bsz, seq, 3 * n_heads * hw)
    kernel = functools.partial(_attn_kernel, lambda_init=lambda_init)
    vec = lambda b, h: (0, 0)
    return pl.pallas_call(
        kernel,
        grid=(bsz, n_heads),
        in_specs=[
            pl.BlockSpec((None, seq, hw), lambda b, h: (b, 0, h)),
            pl.BlockSpec((None, seq, hw), lambda b, h: (b, 0, n_heads + h)),
            pl.BlockSpec((None, seq, hw), lambda b, h: (b, 0, 2 * n_heads + h)),
            pl.BlockSpec((None, 2, t, t), lambda b, h: (h, 0, 0, 0)),
            pl.BlockSpec(memory_space=pltpu.SMEM),
            pl.BlockSpec((1, DIFF_HEAD_DIM), vec),
            pl.BlockSpec((1, DIFF_HEAD_DIM), vec),
            pl.BlockSpec((1, DIFF_HEAD_DIM), vec),
            pl.BlockSpec((1, DIFF_HEAD_DIM), vec),
            pl.BlockSpec((1, hw), vec),
        ],
        out_specs=pl.BlockSpec((None, seq, hw), lambda b, h: (b, 0, h)),
        out_shape=jax.ShapeDtypeStruct((bsz, seq, n_heads * hw), BF16),
        scratch_shapes=[pltpu.VMEM((seq, 2 * hw), BF16)],
        compiler_params=_params("arbitrary", "arbitrary"),
        name="diff_attention",
    )(qkv3, qkv3, qkv3, bias_tiles, rel_bias, lq1[None, :], lk1[None, :], lq2[None, :], lk2[None, :],
      subln_w[None, :])


def kernel(x, a_in_proj, a_conv_w, a_conv_b, a_dt_bias, a_A_log, a_D, a_gate_norm_w, a_out_proj, kv_norm_w, w_k, w_v, rel_bias, b_w_q, b_lambda_q1, b_lambda_k1, b_lambda_q2, b_lambda_k2, b_subln_w, b_out_proj, norm_pre_mix, norm_post_mix, norm_pre_mlp, norm_post_mlp, mlp_w1, mlp_w2):
    bsz, seq, d = x.shape
    m = bsz * seq
    d_inner = a_out_proj.shape[1]
    n_ssm_heads = a_A_log.shape[1]
    n_zx = 2 * d_inner + 2 * N_GROUPS * D_STATE
    n_heads = d // (2 * DIFF_HEAD_DIM)
    x2d = x.reshape(m, d)

    tm = min(1024, m)
    tm_mlp = min(512, m)

    w_in = a_in_proj[0]
    w_zx = w_in[:, :n_zx].astype(BF16)
    w_dt = jnp.pad(w_in[:, n_zx:], ((0, 0), (0, LANES - n_ssm_heads))).astype(BF16)
    dt_bias = jnp.pad(a_dt_bias[0], (0, LANES - n_ssm_heads))[None, :]
    zx, dt = _in_proj(x2d, norm_pre_mix[0][None, :], w_zx, w_dt, dt_bias, tm=tm, tn=min(1024, n_zx))
    g = _ssd(zx, dt, a_conv_w[0], a_conv_b[0], a_A_log[0], a_D[0], a_gate_norm_w[0],
             bsz=bsz, seq=seq, d_inner=d_inner)
    h = _out_proj(g.reshape(m, d_inner), a_out_proj[0].astype(BF16), x2d, norm_post_mix[0][None, :],
                  tm=tm_mlp, tk=min(1024, d_inner))
    h = _mlp(h, norm_pre_mlp[0][None, :], mlp_w1[0].astype(BF16), mlp_w2[0].astype(BF16),
             norm_post_mlp[0][None, :], tm=tm_mlp, tf=min(1024, mlp_w1.shape[2]))

    scale = DIFF_HEAD_DIM ** -0.5
    w_qkv = jnp.concatenate([b_w_q[0] * scale, w_k, w_v], axis=1).astype(BF16)
    qkv = _qkv(h, norm_pre_mix[1][None, :], kv_norm_w[None, :], w_qkv, tm=tm, tn=min(1024, d))
    lambda_init = 0.8 - 0.6 * math.exp(-0.3 * 1)
    bias = _bias_tiles(rel_bias, n_heads, ATTN_BLOCK)
    o = _attention(qkv, bias, rel_bias, b_lambda_q1[0], b_lambda_k1[0], b_lambda_q2[0], b_lambda_k2[0],
                   b_subln_w[0], bsz=bsz, seq=seq, n_heads=n_heads, lambda_init=lambda_init)
    h = _out_proj(o.reshape(m, d), b_out_proj[0].astype(BF16), h, norm_post_mix[1][None, :],
                  tm=tm_mlp, tk=min(1024, d))
    h = _mlp(h, norm_pre_mlp[1][None, :], mlp_w1[1].astype(BF16), mlp_w2[1].astype(BF16),
             norm_post_mlp[1][None, :], tm=tm_mlp, tf=min(1024, mlp_w1.shape[2]))
    return h.reshape(bsz, seq, d)
```

```python
import functools
import math

import numpy as np
import jax
import jax.numpy as jnp
from jax import lax
from jax.experimental import pallas as pl
from jax.experimental.pallas import tpu as pltpu

SSM_HEAD_DIM = 64
D_STATE = 128
N_GROUPS = 8
CONV_W = 4
CHUNK = 128
DIFF_HEAD_DIM = 64
NUM_BUCKETS = 32
MAX_DISTANCE = 128
NORM_EPS = 1e-6
GATE_NORM_EPS = 1e-5
SUBLN_EPS = 1e-5

LANES = 128
CONV_TAIL_ROWS = 8
ATTN_BLOCK = 256
VMEM_LIMIT_BYTES = 56 * 1024 * 1024

BF16 = jnp.bfloat16
F32 = jnp.float32


def _params(*sem):
    return pltpu.CompilerParams(dimension_semantics=sem, vmem_limit_bytes=VMEM_LIMIT_BYTES)


def _rms_scale(xf, eps):
    return xf * lax.rsqrt(jnp.mean(xf * xf, axis=-1, keepdims=True) + eps)


def _silu(v):
    return v * (1.0 / (1.0 + jnp.exp(-v)))


def _dot(a, b):
    return jnp.dot(a, b, preferred_element_type=F32)


def _in_proj_kernel(x_ref, nw_ref, w_ref, wdt_ref, dtb_ref, zx_ref, dt_ref, u_ref):
    j = pl.program_id(1)

    @pl.when(j == 0)
    def _():
        xf = x_ref[...]
        u = (_rms_scale(xf, NORM_EPS) * nw_ref[...]).astype(BF16)
        u_ref[...] = u
        raw = _dot(u, wdt_ref[...]) + dtb_ref[...]
        dt_ref[...] = jnp.maximum(raw, 0.0) + jnp.log1p(jnp.exp(-jnp.abs(raw)))

    zx_ref[...] = _dot(u_ref[...], w_ref[...]).astype(zx_ref.dtype)


def _in_proj(x2d, norm_w, w_zx, w_dt, dt_bias, *, tm, tn):
    m, d = x2d.shape
    n = w_zx.shape[1]
    return pl.pallas_call(
        _in_proj_kernel,
        grid=(m // tm, n // tn),
        in_specs=[
            pl.BlockSpec((tm, d), lambda i, j: (i, 0)),
            pl.BlockSpec((1, d), lambda i, j: (0, 0)),
            pl.BlockSpec((d, tn), lambda i, j: (0, j)),
            pl.BlockSpec((d, LANES), lambda i, j: (0, 0)),
            pl.BlockSpec((1, LANES), lambda i, j: (0, 0)),
        ],
        out_specs=[
            pl.BlockSpec((tm, tn), lambda i, j: (i, j)),
            pl.BlockSpec((tm, LANES), lambda i, j: (i, 0)),
        ],
        out_shape=[
            jax.ShapeDtypeStruct((m, n), F32),
            jax.ShapeDtypeStruct((m, LANES), F32),
        ],
        scratch_shapes=[pltpu.VMEM((tm, d), BF16)],
        compiler_params=_params("arbitrary", "arbitrary"),
        name="in_proj",
    )(x2d, norm_w, w_zx, w_dt, dt_bias)


def _split3(v):
    hi = v.astype(BF16)
    r1 = v - hi.astype(F32)
    mid = r1.astype(BF16)
    lo = (r1 - mid.astype(F32)).astype(BF16)
    return hi, mid, lo


def _ssd_kernel(z_ref, x_ref, bc_ref, dt_ref, cwx_ref, cbx_ref, cwbc_ref, cbbc_ref,
                alog_ref, dskip_ref, gnw_ref, g_ref,
                state_ref, xtail_ref, bctail_ref, cumT_ref, dtT_ref, wT_ref, *, heads_per_group):
    c = pl.program_id(1)
    q = CHUNK
    gp = heads_per_group * SSM_HEAD_DIM
    n_bc = N_GROUPS * D_STATE

    @pl.when(c == 0)
    def _():
        state_ref[...] = jnp.zeros_like(state_ref)
        xtail_ref[...] = jnp.zeros_like(xtail_ref)
        bctail_ref[...] = jnp.zeros_like(bctail_ref)

    dt = dt_ref[...]
    a = dt * (-jnp.exp(alog_ref[...]))
    row = lax.broadcasted_iota(jnp.int32, (q, q), 0)
    col = lax.broadcasted_iota(jnp.int32, (q, q), 1)
    tril = (col <= row).astype(BF16)
    a_hi, a_mid, a_lo = _split3(a)
    cum = _dot(tril, a_hi) + _dot(tril, a_mid) + _dot(tril, a_lo)
    cumT = cum.T
    dtT = dt.T
    cum_last = cumT[:, q - 1:q]
    cumT_ref[...] = cumT
    dtT_ref[...] = dtT
    wT_ref[...] = jnp.exp(cum_last - cumT) * dtT

    causal_T = row <= col

    def conv_silu(tail_ref, cur_ref, w_ref, b_ref, start, width):
        ext = jnp.concatenate([tail_ref[:, pl.ds(start, width)], cur_ref[:, pl.ds(start, width)]],
                              axis=0)
        w = w_ref[:, pl.ds(start, width)]
        acc = b_ref[:, pl.ds(start, width)] + w[CONV_W - 1:CONV_W, :] * ext[CONV_TAIL_ROWS:, :]
        for s in range(1, CONV_W):
            shifted = pltpu.roll(ext, s, axis=0)[CONV_TAIL_ROWS:, :]
            acc = acc + w[CONV_W - 1 - s:CONV_W - s, :] * shifted
        return _silu(acc)

    def group_body(g, carry):
        xs = conv_silu(xtail_ref, x_ref, cwx_ref, cbx_ref, pl.multiple_of(g * gp, gp), gp)
        bg = conv_silu(bctail_ref, bc_ref, cwbc_ref, cbbc_ref,
                       pl.multiple_of(g * D_STATE, D_STATE), D_STATE)
        cg = conv_silu(bctail_ref, bc_ref, cwbc_ref, cbbc_ref,
                       pl.multiple_of(n_bc + g * D_STATE, D_STATE), D_STATE)
        bg16 = bg.astype(BF16)
        cgT16 = cg.T.astype(BF16)
        cbT = _dot(bg16, cgT16)
        xT = xs.T
        h0 = pl.multiple_of(g * heads_per_group, heads_per_group)
        cum_g = cumT_ref[pl.ds(h0, heads_per_group), :]
        dt_g = dtT_ref[pl.ds(h0, heads_per_group), :]
        w_g = wT_ref[pl.ds(h0, heads_per_group), :]
        st = state_ref[g]
        inter = _dot(st.astype(BF16), cgT16)
        y_parts = []
        xw_parts = []
        dec_parts = []
        for r in range(heads_per_group):
            lo, hi = r * SSM_HEAD_DIM, (r + 1) * SSM_HEAD_DIM
            cum_row = cum_g[r:r + 1, :]
            cum_col = jnp.broadcast_to(cum_row, (q, q)).T
            seg = jnp.where(causal_T, cum_row - cum_col, -jnp.inf)
            mT = (cbT * jnp.exp(seg)).astype(BF16)
            xh = xT[lo:hi, :]
            xdt = (xh * dt_g[r:r + 1, :]).astype(BF16)
            y = _dot(xdt, mT)
            y = y + inter[lo:hi, :] * jnp.exp(cum_row)
            y = y + dskip_ref[h0 + r] * xh
            y_parts.append(y)
            xw_parts.append((xh * w_g[r:r + 1, :]).astype(BF16))
            dec = jnp.exp(cum_row[:, q - 1:q])
            dec_parts.append(jnp.broadcast_to(dec, (SSM_HEAD_DIM, D_STATE)))
        yT = jnp.concatenate(y_parts, axis=0)
        xw = jnp.concatenate(xw_parts, axis=0)
        decay = jnp.concatenate(dec_parts, axis=0)
        state_ref[g] = st * decay + _dot(xw, bg16)
        y_tm = yT.T
        z = z_ref[:, pl.ds(pl.multiple_of(g * gp, gp), gp)]
        gated = y_tm * _silu(z)
        gated = _rms_scale(gated, GATE_NORM_EPS) * gnw_ref[:, pl.ds(pl.multiple_of(g * gp, gp), gp)]
        g_ref[:, pl.ds(pl.multiple_of(g * gp, gp), gp)] = gated.astype(g_ref.dtype)
        return carry

    lax.fori_loop(0, N_GROUPS, group_body, 0)

    xtail_ref[...] = x_ref[q - CONV_TAIL_ROWS:, :]
    bctail_ref[...] = bc_ref[q - CONV_TAIL_ROWS:, :]


def _ssd(zx, dt, conv_w, conv_b, a_log, d_skip, gate_norm_w, *, bsz, seq, d_inner):
    n_heads = d_inner // SSM_HEAD_DIM
    hpg = n_heads // N_GROUPS
    n_bc = N_GROUPS * D_STATE
    nc = seq // CHUNK
    zx3 = zx.reshape(bsz, seq, zx.shape[-1])
    dt3 = dt.reshape(bsz, seq, LANES)
    blk_x = d_inner // (2 * n_bc)
    assert d_inner % (2 * n_bc) == 0
    cw_x, cw_bc = conv_w[:, :d_inner], conv_w[:, d_inner:]
    cb_x, cb_bc = conv_b[None, :d_inner], conv_b[None, d_inner:]
    alog = jnp.pad(a_log, (0, LANES - n_heads))[None, :]
    kernel = functools.partial(_ssd_kernel, heads_per_group=hpg)
    const2 = lambda b, c: (0, 0)
    return pl.pallas_call(
        kernel,
        grid=(bsz, nc),
        in_specs=[
            pl.BlockSpec((None, CHUNK, d_inner), lambda b, c: (b, c, 0)),
            pl.BlockSpec((None, CHUNK, d_inner), lambda b, c: (b, c, 1)),
            pl.BlockSpec((None, CHUNK, 2 * n_bc), lambda b, c: (b, c, 2 * blk_x)),
            pl.BlockSpec((None, CHUNK, LANES), lambda b, c: (b, c, 0)),
            pl.BlockSpec((CONV_W, d_inner), const2),
            pl.BlockSpec((1, d_inner), const2),
            pl.BlockSpec((CONV_W, 2 * n_bc), const2),
            pl.BlockSpec((1, 2 * n_bc), const2),
            pl.BlockSpec((1, LANES), const2),
            pl.BlockSpec(memory_space=pltpu.SMEM),
            pl.BlockSpec((1, d_inner), const2),
        ],
        out_specs=pl.BlockSpec((None, CHUNK, d_inner), lambda b, c: (b, c, 0)),
        out_shape=jax.ShapeDtypeStruct((bsz, seq, d_inner), BF16),
        scratch_shapes=[
            pltpu.VMEM((N_GROUPS, hpg * SSM_HEAD_DIM, D_STATE), F32),
            pltpu.VMEM((CONV_TAIL_ROWS, d_inner), F32),
            pltpu.VMEM((CONV_TAIL_ROWS, 2 * n_bc), F32),
            pltpu.VMEM((LANES, CHUNK), F32),
            pltpu.VMEM((LANES, CHUNK), F32),
            pltpu.VMEM((LANES, CHUNK), F32),
        ],
        compiler_params=_params("arbitrary", "arbitrary"),
        name="ssd",
    )(zx3, zx3, zx3, dt3, cw_x, cb_x, cw_bc, cb_bc, alog, d_skip, gate_norm_w[None, :])


def _out_proj_kernel(a_ref, w_ref, res_ref, nw_ref, o_ref):
    k = pl.program_id(1)

    @pl.when(k == 0)
    def _():
        o_ref[...] = jnp.zeros_like(o_ref)

    o_ref[...] += _dot(a_ref[...], w_ref[...])

    @pl.when(k == pl.num_programs(1) - 1)
    def _():
        o_ref[...] = res_ref[...] + _rms_scale(o_ref[...], NORM_EPS) * nw_ref[...]


def _out_proj(a, w, res, norm_w, *, tm, tk):
    m, kdim = a.shape
    d = w.shape[1]
    return pl.pallas_call(
        _out_proj_kernel,
        grid=(m // tm, kdim // tk),
        in_specs=[
            pl.BlockSpec((tm, tk), lambda i, k: (i, k)),
            pl.BlockSpec((tk, d), lambda i, k: (k, 0)),
            pl.BlockSpec((tm, d), lambda i, k: (i, 0)),
            pl.BlockSpec((1, d), lambda i, k: (0, 0)),
        ],
        out_specs=pl.BlockSpec((tm, d), lambda i, k: (i, 0)),
        out_shape=jax.ShapeDtypeStruct((m, d), F32),
        compiler_params=_params("arbitrary", "arbitrary"),
        name="out_proj",
    )(a, w, res, norm_w)


def _mlp_kernel(h_ref, pre_ref, w1_ref, w2_ref, post_ref, o_ref, u_ref):
    f = pl.program_id(1)

    @pl.when(f == 0)
    def _():
        u_ref[...] = (_rms_scale(h_ref[...], NORM_EPS) * pre_ref[...]).astype(BF16)
        o_ref[...] = jnp.zeros_like(o_ref)

    a = jnp.maximum(_dot(u_ref[...], w1_ref[...]), 0.0)
    o_ref[...] += _dot((a * a).astype(BF16), w2_ref[...])

    @pl.when(f == pl.num_programs(1) - 1)
    def _():
        o_ref[...] = h_ref[...] + _rms_scale(o_ref[...], NORM_EPS) * post_ref[...]


def _mlp(h, pre_w, w1, w2, post_w, *, tm, tf):
    m, d = h.shape
    ff = w1.shape[1]
    return pl.pallas_call(
        _mlp_kernel,
        grid=(m // tm, ff // tf),
        in_specs=[
            pl.BlockSpec((tm, d), lambda i, f: (i, 0)),
            pl.BlockSpec((1, d), lambda i, f: (0, 0)),
            pl.BlockSpec((d, tf), lambda i, f: (0, f)),
            pl.BlockSpec((tf, d), lambda i, f: (f, 0)),
            pl.BlockSpec((1, d), lambda i, f: (0, 0)),
        ],
        out_specs=pl.BlockSpec((tm, d), lambda i, f: (i, 0)),
        out_shape=jax.ShapeDtypeStruct((m, d), F32),
        scratch_shapes=[pltpu.VMEM((tm, d), BF16)],
        compiler_params=_params("arbitrary", "arbitrary"),
        name="mlp",
    )(h, pre_w, w1, w2, post_w)


def _qkv_kernel(h_ref, qn_ref, kvn_ref, w_ref, o_ref, u_ref, *, n_q_tiles):
    j = pl.program_id(1)

    @pl.when(j == 0)
    def _():
        xhat = _rms_scale(h_ref[...], NORM_EPS)
        u_ref[0] = (xhat * qn_ref[...]).astype(BF16)
        u_ref[1] = (xhat * kvn_ref[...]).astype(BF16)

    sel = (j >= n_q_tiles).astype(jnp.int32)
    o_ref[...] = _dot(u_ref[sel], w_ref[...]).astype(o_ref.dtype)


def _qkv(h, q_norm_w, kv_norm_w, w_qkv, *, tm, tn):
    m, d = h.shape
    n = w_qkv.shape[1]
    kernel = functools.partial(_qkv_kernel, n_q_tiles=(n // 3) // tn)
    return pl.pallas_call(
        kernel,
        grid=(m // tm, n // tn),
        in_specs=[
            pl.BlockSpec((tm, d), lambda i, j: (i, 0)),
            pl.BlockSpec((1, d), lambda i, j: (0, 0)),
            pl.BlockSpec((1, d), lambda i, j: (0, 0)),
            pl.BlockSpec((d, tn), lambda i, j: (0, j)),
        ],
        out_specs=pl.BlockSpec((tm, tn), lambda i, j: (i, j)),
        out_shape=jax.ShapeDtypeStruct((m, n), BF16),
        scratch_shapes=[pltpu.VMEM((2, tm, d), BF16)],
        compiler_params=_params("arbitrary", "arbitrary"),
        name="qkv",
    )(h, q_norm_w, kv_norm_w, w_qkv)


def _t5_bucket_np(dist):
    n = np.maximum(dist, 0)
    max_exact = NUM_BUCKETS // 2
    nf = np.maximum(n, 1).astype(np.float32)
    large = max_exact + (np.log(nf / np.float32(max_exact)) / np.float32(math.log(MAX_DISTANCE / max_exact))
                         * np.float32(NUM_BUCKETS - max_exact)).astype(np.int32)
    large = np.minimum(large, NUM_BUCKETS - 1)
    return np.where(n < max_exact, n, large).astype(np.int32)


def _bucket_tiles(t):
    i = np.arange(t)[:, None]
    j = np.arange(t)[None, :]
    diag = np.where(i >= j, _t5_bucket_np(i - j), -1)
    left = _t5_bucket_np(t + i - j)
    return np.stack([diag, left]).astype(np.int32)


def _bias_kernel(idx_ref, rel_ref, o_ref):
    h = pl.program_id(0)
    idx = idx_ref[...]
    acc = jnp.where(idx < 0, -jnp.inf, 0.0).astype(F32)
    for b in range(NUM_BUCKETS):
        acc = jnp.where(idx == b, rel_ref[b, h], acc)
    o_ref[...] = acc


def _bias_tiles(rel_bias, n_heads, t):
    idx = jnp.asarray(_bucket_tiles(t))
    return pl.pallas_call(
        _bias_kernel,
        grid=(n_heads,),
        in_specs=[
            pl.BlockSpec((2, t, t), lambda h: (0, 0, 0)),
            pl.BlockSpec(memory_space=pltpu.SMEM),
        ],
        out_specs=pl.BlockSpec((None, 2, t, t), lambda h: (h, 0, 0, 0)),
        out_shape=jax.ShapeDtypeStruct((n_heads, 2, t, t), F32),
        compiler_params=_params("arbitrary"),
        name="bias_tiles",
    )(idx, rel_bias)


def _attn_kernel(q_ref, k_ref, v_ref, bias_ref, rel_ref, lq1_ref, lk1_ref, lq2_ref, lk2_ref, sw_ref,
                 o_ref, vext_ref, *, lambda_init):
    h = pl.program_id(1)
    t = ATTN_BLOCK
    dh = DIFF_HEAD_DIM
    hw = 2 * dh
    seq = q_ref.shape[0]

    vext_ref[:, 0:hw] = v_ref[...]
    vext_ref[:, hw:] = jnp.ones((seq, hw), BF16)

    lam = (jnp.exp(jnp.sum(lq1_ref[...] * lk1_ref[...], axis=-1, keepdims=True))
           - jnp.exp(jnp.sum(lq2_ref[...] * lk2_ref[...], axis=-1, keepdims=True))
           + lambda_init)
    far_bias = rel_ref[NUM_BUCKETS - 1, h]
    diag = bias_ref[0]
    diag2 = jnp.concatenate([diag, diag], axis=0)
    left = bias_ref[1]
    left2 = jnp.concatenate([left, left], axis=0)

    def scores(i):
        q = q_ref[i * t:(i + 1) * t, :]
        lane = lax.broadcasted_iota(jnp.int32, q.shape, 1)
        zero = jnp.zeros_like(q)
        qq = jnp.concatenate([jnp.where(lane < dh, q, zero), jnp.where(lane >= dh, q, zero)], axis=0)
        return lax.dot_general(qq, k_ref[0:(i + 1) * t, :], (((1,), (1,)), ((), ())),
                               preferred_element_type=F32)

    n_q = seq // t
    s_next = scores(0)
    for i in range(n_q):
        n_kv = (i + 1) * t
        s = s_next
        if i + 1 < n_q:
            s_next = scores(i + 1)
        s_diag = s[:, n_kv - t:] + diag2
        m = jnp.max(s_diag, axis=-1, keepdims=True)
        parts = []
        if i >= 1:
            s_left = s[:, n_kv - 2 * t:n_kv - t] + left2
            m = jnp.maximum(m, jnp.max(s_left, axis=-1, keepdims=True))
        if i >= 2:
            s_far = s[:, 0:n_kv - 2 * t]
            m = jnp.maximum(m, jnp.max(s_far, axis=-1, keepdims=True) + far_bias)
            parts.append(jnp.exp(s_far - (m - far_bias)).astype(BF16))
        if i >= 1:
            parts.append(jnp.exp(s_left - m).astype(BF16))
        parts.append(jnp.exp(s_diag - m).astype(BF16))
        p = parts[0] if len(parts) == 1 else jnp.concatenate(parts, axis=1)
        acc = _dot(p, vext_ref[0:n_kv, :])
        out = acc[:, 0:hw] / acc[:, hw:]
        o = out[0:t, :] - lam * out[t:, :]
        o = _rms_scale(o, SUBLN_EPS) * sw_ref[...] * (1.0 - lambda_init)
        o_ref[i * t:(i + 1) * t, :] = o.astype(o_ref.dtype)


def _attention(qkv, bias_tiles, rel_bias, lq1, lk1, lq2, lk2, subln_w, *, bsz, seq, n_heads, lambda_init):
    t = ATTN_BLOCK
    hw = 2 * DIFF_HEAD_DIM
    qkv3 = qkv.reshape(bsz, seq, 3 * n_heads * hw)
    kernel = functools.partial(_attn_kernel, lambda_init=lambda_init)
    vec = lambda b, h: (0, 0)
    return pl.pallas_call(
        kernel,
        grid=(bsz, n_heads),
        in_specs=[
            pl.BlockSpec((None, seq, hw), lambda b, h: (b, 0, h)),
            pl.BlockSpec((None, seq, hw), lambda b, h: (b, 0, n_heads + h)),
            pl.BlockSpec((None, seq, hw), lambda b, h: (b, 0, 2 * n_heads + h)),
            pl.BlockSpec((None, 2, t, t), lambda b, h: (h, 0, 0, 0)),
            pl.BlockSpec(memory_space=pltpu.SMEM),
            pl.BlockSpec((1, DIFF_HEAD_DIM), vec),
            pl.BlockSpec((1, DIFF_HEAD_DIM), vec),
            pl.BlockSpec((1, DIFF_HEAD_DIM), vec),
            pl.BlockSpec((1, DIFF_HEAD_DIM), vec),
            pl.BlockSpec((1, hw), vec),
        ],
        out_specs=pl.BlockSpec((None, seq, hw), lambda b, h: (b, 0, h)),
        out_shape=jax.ShapeDtypeStruct((bsz, seq, n_heads * hw), BF16),
        scratch_shapes=[pltpu.VMEM((seq, 2 * hw), BF16)],
        compiler_params=_params("arbitrary", "arbitrary"),
        name="diff_attention",
    )(qkv3, qkv3, qkv3, bias_tiles, rel_bias, lq1[None, :], lk1[None, :], lq2[None, :], lk2[None, :],
      subln_w[None, :])


def kernel(x, a_in_proj, a_conv_w, a_conv_b, a_dt_bias, a_A_log, a_D, a_gate_norm_w, a_out_proj, kv_norm_w, w_k, w_v, rel_bias, b_w_q, b_lambda_q1, b_lambda_k1, b_lambda_q2, b_lambda_k2, b_subln_w, b_out_proj, norm_pre_mix, norm_post_mix, norm_pre_mlp, norm_post_mlp, mlp_w1, mlp_w2):
    bsz, seq, d = x.shape
    m = bsz * seq
    d_inner = a_out_proj.shape[1]
    n_ssm_heads = a_A_log.shape[1]
    n_zx = 2 * d_inner + 2 * N_GROUPS * D_STATE
    n_heads = d // (2 * DIFF_HEAD_DIM)
    x2d = x.reshape(m, d)

    tm = min(1024, m)
    tm_mlp = min(512, m)

    w_in = a_in_proj[0]
    w_zx = w_in[:, :n_zx].astype(BF16)
    w_dt = jnp.pad(w_in[:, n_zx:], ((0, 0), (0, LANES - n_ssm_heads))).astype(BF16)
    dt_bias = jnp.pad(a_dt_bias[0], (0, LANES - n_ssm_heads))[None, :]
    zx, dt = _in_proj(x2d, norm_pre_mix[0][None, :], w_zx, w_dt, dt_bias, tm=tm, tn=min(1024, n_zx))
    g = _ssd(zx, dt, a_conv_w[0], a_conv_b[0], a_A_log[0], a_D[0], a_gate_norm_w[0],
             bsz=bsz, seq=seq, d_inner=d_inner)
    h = _out_proj(g.reshape(m, d_inner), a_out_proj[0].astype(BF16), x2d, norm_post_mix[0][None, :],
                  tm=tm_mlp, tk=min(1024, d_inner))
    h = _mlp(h, norm_pre_mlp[0][None, :], mlp_w1[0].astype(BF16), mlp_w2[0].astype(BF16),
             norm_post_mlp[0][None, :], tm=tm_mlp, tf=min(1024, mlp_w1.shape[2]))

    scale = DIFF_HEAD_DIM ** -0.5
    w_qkv = jnp.concatenate([b_w_q[0] * scale, w_k, w_v], axis=1).astype(BF16)
    qkv = _qkv(h, norm_pre_mix[1][None, :], kv_norm_w[None, :], w_qkv, tm=tm, tn=min(1024, d))
    lambda_init = 0.8 - 0.6 * math.exp(-0.3 * 1)
    bias = _bias_tiles(rel_bias, n_heads, ATTN_BLOCK)
    o = _attention(qkv, bias, rel_bias, b_lambda_q1[0], b_lambda_k1[0], b_lambda_q2[0], b_lambda_k2[0],
                   b_subln_w[0], bsz=bsz, seq=seq, n_heads=n_heads, lambda_init=lambda_init)
    h = _out_proj(o.reshape(m, d), b_out_proj[0].astype(BF16), h, norm_post_mix[1][None, :],
                  tm=tm_mlp, tk=min(1024, d))
    h = _mlp(h, norm_pre_mlp[1][None, :], mlp_w1[1].astype(BF16), mlp_w2[1].astype(BF16),
             norm_post_mlp[1][None, :], tm=tm_mlp, tf=min(1024, mlp_w1.shape[2]))
    return h.reshape(bsz, seq, d)
```

```python
import functools
import math

import numpy as np
import jax
import jax.numpy as jnp
from jax import lax
from jax.experimental import pallas as pl
from jax.experimental.pallas import tpu as pltpu

SSM_HEAD_DIM = 64
D_STATE = 128
N_GROUPS = 8
CONV_W = 4
CHUNK = 128
DIFF_HEAD_DIM = 64
NUM_BUCKETS = 32
MAX_DISTANCE = 128
NORM_EPS = 1e-6
GATE_NORM_EPS = 1e-5
SUBLN_EPS = 1e-5
LOG2_E = math.log2(math.e)

LANES = 128
CONV_TAIL_ROWS = 8
ATTN_BLOCK = 256
VMEM_LIMIT_BYTES = 56 * 1024 * 1024

BF16 = jnp.bfloat16
F32 = jnp.float32


def _params(*sem):
    return pltpu.CompilerParams(dimension_semantics=sem, vmem_limit_bytes=VMEM_LIMIT_BYTES)


def _rms_scale(xf, eps):
    return xf * lax.rsqrt(jnp.mean(xf * xf, axis=-1, keepdims=True) + eps)


def _silu(v):
    return v * (1.0 / (1.0 + jnp.exp(-v)))


def _dot(a, b):
    return jnp.dot(a, b, preferred_element_type=F32)


def _in_proj_kernel(x_ref, nw_ref, w_ref, wdt_ref, dtb_ref, zx_ref, dt_ref, u_ref):
    j = pl.program_id(1)

    @pl.when(j == 0)
    def _():
        xf = x_ref[...]
        u = (_rms_scale(xf, NORM_EPS) * nw_ref[...]).astype(BF16)
        u_ref[...] = u
        raw = _dot(u, wdt_ref[...]) + dtb_ref[...]
        dt_ref[...] = jnp.maximum(raw, 0.0) + jnp.log1p(jnp.exp(-jnp.abs(raw)))

    zx_ref[...] = _dot(u_ref[...], w_ref[...]).astype(zx_ref.dtype)


def _in_proj(x2d, norm_w, w_zx, w_dt, dt_bias, *, n, tm, tn):
    m, d = x2d.shape
    assert n % tn == 0
    return pl.pallas_call(
        _in_proj_kernel,
        grid=(m // tm, n // tn),
        in_specs=[
            pl.BlockSpec((tm, d), lambda i, j: (i, 0)),
            pl.BlockSpec((1, d), lambda i, j: (0, 0)),
            pl.BlockSpec((d, tn), lambda i, j: (0, j)),
            pl.BlockSpec((d, LANES), lambda i, j: (0, 0)),
            pl.BlockSpec((1, LANES), lambda i, j: (0, 0)),
        ],
        out_specs=[
            pl.BlockSpec((tm, tn), lambda i, j: (i, j)),
            pl.BlockSpec((tm, LANES), lambda i, j: (i, 0)),
        ],
        out_shape=[
            jax.ShapeDtypeStruct((m, n), F32),
            jax.ShapeDtypeStruct((m, LANES), F32),
        ],
        scratch_shapes=[pltpu.VMEM((tm, d), BF16)],
        compiler_params=_params("arbitrary", "arbitrary"),
        name="in_proj",
    )(x2d, norm_w, w_zx, w_dt, dt_bias)


def _split3(v):
    hi = v.astype(BF16)
    r1 = v - hi.astype(F32)
    mid = r1.astype(BF16)
    lo = (r1 - mid.astype(F32)).astype(BF16)
    return hi, mid, lo


def _ssd_kernel(z_ref, x_ref, bc_ref, dt_ref, cwx_ref, cbx_ref, cwbc_ref, cbbc_ref,
                alog_ref, dskip_ref, gnw_ref, g_ref,
                state_ref, xtail_ref, bctail_ref, cumT_ref, dtT_ref, wT_ref, *, heads_per_group):
    c = pl.program_id(1)
    q = CHUNK
    gp = heads_per_group * SSM_HEAD_DIM
    n_bc = N_GROUPS * D_STATE

    @pl.when(c == 0)
    def _():
        state_ref[...] = jnp.zeros_like(state_ref)
        xtail_ref[...] = jnp.zeros_like(xtail_ref)
        bctail_ref[...] = jnp.zeros_like(bctail_ref)

    dt = dt_ref[...]
    a = dt * (-jnp.exp(alog_ref[...]))
    row = lax.broadcasted_iota(jnp.int32, (q, q), 0)
    col = lax.broadcasted_iota(jnp.int32, (q, q), 1)
    tril = (col <= row).astype(BF16)
    a_hi, a_mid, a_lo = _split3(a)
    cum = _dot(tril, a_hi) + _dot(tril, a_mid) + _dot(tril, a_lo)
    cumT = cum.T * LOG2_E
    dtT = dt.T
    cum_last = cumT[:, q - 1:q]
    cumT_ref[...] = cumT
    dtT_ref[...] = dtT
    wT_ref[...] = jnp.exp2(cum_last - cumT) * dtT

    causal_T = row <= col

    def conv_silu(tail_ref, cur_ref, w_ref, b_ref, start, width):
        cols = slice(start, start + width)
        ext = jnp.concatenate([tail_ref[:, cols], cur_ref[:, cols]], axis=0)
        w = w_ref[:, cols]
        acc = b_ref[:, cols] + w[CONV_W - 1:CONV_W, :] * ext[CONV_TAIL_ROWS:, :]
        for s in range(1, CONV_W):
            shifted = pltpu.roll(ext, s, axis=0)[CONV_TAIL_ROWS:, :]
            acc = acc + w[CONV_W - 1 - s:CONV_W - s, :] * shifted
        return _silu(acc)

    for g in range(N_GROUPS):
        xs = conv_silu(xtail_ref, x_ref, cwx_ref, cbx_ref, g * gp, gp)
        bg16 = conv_silu(bctail_ref, bc_ref, cwbc_ref, cbbc_ref, g * D_STATE, D_STATE).astype(BF16)
        cgT16 = conv_silu(bctail_ref, bc_ref, cwbc_ref, cbbc_ref,
                          n_bc + g * D_STATE, D_STATE).T.astype(BF16)
        cbT = _dot(bg16, cgT16)
        xT = xs.T
        h0 = g * heads_per_group
        cum_g = cumT_ref[h0:h0 + heads_per_group, :]
        dt_g = dtT_ref[h0:h0 + heads_per_group, :]
        w_g = wT_ref[h0:h0 + heads_per_group, :]
        st = state_ref[g]
        inter = _dot(st.astype(BF16), cgT16)
        y_parts = []
        xw_parts = []
        dec_parts = []
        for r in range(heads_per_group):
            lo, hi = r * SSM_HEAD_DIM, (r + 1) * SSM_HEAD_DIM
            cum_row = cum_g[r:r + 1, :]
            cum_col = jnp.broadcast_to(cum_row, (q, q)).T
            seg = jnp.where(causal_T, cum_row - cum_col, -jnp.inf)
            mT = (cbT * jnp.exp2(seg)).astype(BF16)
            xh = xT[lo:hi, :]
            xdt = (xh * dt_g[r:r + 1, :]).astype(BF16)
            y = _dot(xdt, mT)
            y = y + inter[lo:hi, :] * jnp.exp2(cum_row)
            y = y + dskip_ref[h0 + r] * xh
            y_parts.append(y)
            xw_parts.append((xh * w_g[r:r + 1, :]).astype(BF16))
            dec = jnp.exp2(cum_row[:, q - 1:q])
            dec_parts.append(jnp.broadcast_to(dec, (SSM_HEAD_DIM, D_STATE)))
        yT = jnp.concatenate(y_parts, axis=0)
        xw = jnp.concatenate(xw_parts, axis=0)
        decay = jnp.concatenate(dec_parts, axis=0)
        state_ref[g] = st * decay + _dot(xw, bg16)
        gated = yT.T * _silu(z_ref[:, g * gp:(g + 1) * gp])
        gated = _rms_scale(gated, GATE_NORM_EPS) * gnw_ref[:, g * gp:(g + 1) * gp]
        g_ref[:, g * gp:(g + 1) * gp] = gated.astype(g_ref.dtype)

    xtail_ref[...] = x_ref[q - CONV_TAIL_ROWS:, :]
    bctail_ref[...] = bc_ref[q - CONV_TAIL_ROWS:, :]


def _ssd(zx, dt, conv_w, conv_b, a_log, d_skip, gate_norm_w, *, bsz, seq, d_inner):
    n_heads = d_inner // SSM_HEAD_DIM
    hpg = n_heads // N_GROUPS
    n_bc = N_GROUPS * D_STATE
    nc = seq // CHUNK
    zx3 = zx.reshape(bsz, seq, zx.shape[-1])
    dt3 = dt.reshape(bsz, seq, LANES)
    blk_x = d_inner // (2 * n_bc)
    assert d_inner % (2 * n_bc) == 0
    cw_x, cw_bc = conv_w[:, :d_inner], conv_w[:, d_inner:]
    cb_x, cb_bc = conv_b[None, :d_inner], conv_b[None, d_inner:]
    alog = jnp.pad(a_log, (0, LANES - n_heads))[None, :]
    kernel = functools.partial(_ssd_kernel, heads_per_group=hpg)
    const2 = lambda b, c: (0, 0)
    return pl.pallas_call(
        kernel,
        grid=(bsz, nc),
        in_specs=[
            pl.BlockSpec((None, CHUNK, d_inner), lambda b, c: (b, c, 0)),
            pl.BlockSpec((None, CHUNK, d_inner), lambda b, c: (b, c, 1)),
            pl.BlockSpec((None, CHUNK, 2 * n_bc), lambda b, c: (b, c, 2 * blk_x)),
            pl.BlockSpec((None, CHUNK, LANES), lambda b, c: (b, c, 0)),
            pl.BlockSpec((CONV_W, d_inner), const2),
            pl.BlockSpec((1, d_inner), const2),
            pl.BlockSpec((CONV_W, 2 * n_bc), const2),
            pl.BlockSpec((1, 2 * n_bc), const2),
            pl.BlockSpec((1, LANES), const2),
            pl.BlockSpec(memory_space=pltpu.SMEM),
            pl.BlockSpec((1, d_inner), const2),
        ],
        out_specs=pl.BlockSpec((None, CHUNK, d_inner), lambda b, c: (b, c, 0)),
        out_shape=jax.ShapeDtypeStruct((bsz, seq, d_inner), BF16),
        scratch_shapes=[
            pltpu.VMEM((N_GROUPS, hpg * SSM_HEAD_DIM, D_STATE), F32),
            pltpu.VMEM((CONV_TAIL_ROWS, d_inner), F32),
            pltpu.VMEM((CONV_TAIL_ROWS, 2 * n_bc), F32),
            pltpu.VMEM((LANES, CHUNK), F32),
            pltpu.VMEM((LANES, CHUNK), F32),
            pltpu.VMEM((LANES, CHUNK), F32),
        ],
        compiler_params=_params("arbitrary", "arbitrary"),
        name="ssd",
    )(zx3, zx3, zx3, dt3, cw_x, cb_x, cw_bc, cb_bc, alog, d_skip, gate_norm_w[None, :])


def _out_proj_kernel(a_ref, w_ref, res_ref, nw_ref, o_ref):
    k = pl.program_id(1)

    @pl.when(k == 0)
    def _():
        o_ref[...] = jnp.zeros_like(o_ref)

    o_ref[...] += _dot(a_ref[...], w_ref[...])

    @pl.when(k == pl.num_programs(1) - 1)
    def _():
        o_ref[...] = res_ref[...] + _rms_scale(o_ref[...], NORM_EPS) * nw_ref[...]


def _out_proj(a, w, res, norm_w, *, tm, tk):
    m, kdim = a.shape
    d = w.shape[1]
    return pl.pallas_call(
        _out_proj_kernel,
        grid=(m // tm, kdim // tk),
        in_specs=[
            pl.BlockSpec((tm, tk), lambda i, k: (i, k)),
            pl.BlockSpec((tk, d), lambda i, k: (k, 0)),
            pl.BlockSpec((tm, d), lambda i, k: (i, 0)),
            pl.BlockSpec((1, d), lambda i, k: (0, 0)),
        ],
        out_specs=pl.BlockSpec((tm, d), lambda i, k: (i, 0)),
        out_shape=jax.ShapeDtypeStruct((m, d), F32),
        compiler_params=_params("arbitrary", "arbitrary"),
        name="out_proj",
    )(a, w, res, norm_w)


def _mlp_kernel(h_ref, pre_ref, w1_ref, w2_ref, post_ref, o_ref, u_ref):
    f = pl.program_id(1)

    @pl.when(f == 0)
    def _():
        u_ref[...] = (_rms_scale(h_ref[...], NORM_EPS) * pre_ref[...]).astype(BF16)
        o_ref[...] = jnp.zeros_like(o_ref)

    a = jnp.maximum(_dot(u_ref[...], w1_ref[...]), 0.0)
    o_ref[...] += _dot((a * a).astype(BF16), w2_ref[...])

    @pl.when(f == pl.num_programs(1) - 1)
    def _():
        o_ref[...] = h_ref[...] + _rms_scale(o_ref[...], NORM_EPS) * post_ref[...]


def _mlp(h, pre_w, w1, w2, post_w, *, tm, tf):
    m, d = h.shape
    ff = w1.shape[1]
    return pl.pallas_call(
        _mlp_kernel,
        grid=(m // tm, ff // tf),
        in_specs=[
            pl.BlockSpec((tm, d), lambda i, f: (i, 0)),
            pl.BlockSpec((1, d), lambda i, f: (0, 0)),
            pl.BlockSpec((d, tf), lambda i, f: (0, f)),
            pl.BlockSpec((tf, d), lambda i, f: (f, 0)),
            pl.BlockSpec((1, d), lambda i, f: (0, 0)),
        ],
        out_specs=pl.BlockSpec((tm, d), lambda i, f: (i, 0)),
        out_shape=jax.ShapeDtypeStruct((m, d), F32),
        scratch_shapes=[pltpu.VMEM((tm, d), BF16)],
        compiler_params=_params("arbitrary", "arbitrary"),
        name="mlp",
    )(h, pre_w, w1, w2, post_w)


def _qkv_kernel(h_ref, qn_ref, kvn_ref, w_ref, o_ref, u_ref, *, n_q_tiles):
    j = pl.program_id(1)

    @pl.when(j == 0)
    def _():
        xhat = _rms_scale(h_ref[...], NORM_EPS)
        u_ref[0] = (xhat * qn_ref[...]).astype(BF16)
        u_ref[1] = (xhat * kvn_ref[...]).astype(BF16)

    sel = (j >= n_q_tiles).astype(jnp.int32)
    o_ref[...] = _dot(u_ref[sel], w_ref[...]).astype(o_ref.dtype)


def _qkv(h, q_norm_w, kv_norm_w, w_qkv, *, tm, tn):
    m, d = h.shape
    n = w_qkv.shape[1]
    kernel = functools.partial(_qkv_kernel, n_q_tiles=(n // 3) // tn)
    return pl.pallas_call(
        kernel,
        grid=(m // tm, n // tn),
        in_specs=[
            pl.BlockSpec((tm, d), lambda i, j: (i, 0)),
            pl.BlockSpec((1, d), lambda i, j: (0, 0)),
            pl.BlockSpec((1, d), lambda i, j: (0, 0)),
            pl.BlockSpec((d, tn), lambda i, j: (0, j)),
        ],
        out_specs=pl.BlockSpec((tm, tn), lambda i, j: (i, j)),
        out_shape=jax.ShapeDtypeStruct((m, n), BF16),
        scratch_shapes=[pltpu.VMEM((2, tm, d), BF16)],
        compiler_params=_params("arbitrary", "arbitrary"),
        name="qkv",
    )(h, q_norm_w, kv_norm_w, w_qkv)


def _t5_bucket_np(dist):
    n = np.maximum(dist, 0)
    max_exact = NUM_BUCKETS // 2
    nf = np.maximum(n, 1).astype(np.float32)
    large = max_exact + (np.log(nf / np.float32(max_exact)) / np.float32(math.log(MAX_DISTANCE / max_exact))
                         * np.float32(NUM_BUCKETS - max_exact)).astype(np.int32)
    large = np.minimum(large, NUM_BUCKETS - 1)
    return np.where(n < max_exact, n, large).astype(np.int32)


def _bucket_tiles(t):
    i = np.arange(t)[:, None]
    j = np.arange(t)[None, :]
    diag = np.where(i >= j, _t5_bucket_np(i - j), -1)
    left = _t5_bucket_np(t + i - j)
    return np.stack([diag, left]).astype(np.int32)


def _bias_kernel(idx_ref, rel_ref, o_ref):
    h = pl.program_id(0)
    idx = idx_ref[...]
    acc = jnp.where(idx < 0, -jnp.inf, 0.0).astype(F32)
    for b in range(NUM_BUCKETS):
        acc = jnp.where(idx == b, rel_ref[b, h], acc)
    o_ref[...] = acc


def _bias_tiles(rel_bias, n_heads, t):
    idx = jnp.asarray(_bucket_tiles(t))
    return pl.pallas_call(
        _bias_kernel,
        grid=(n_heads,),
        in_specs=[
            pl.BlockSpec((2, t, t), lambda h: (0, 0, 0)),
            pl.BlockSpec(memory_space=pltpu.SMEM),
        ],
        out_specs=pl.BlockSpec((None, 2, t, t), lambda h: (h, 0, 0, 0)),
        out_shape=jax.ShapeDtypeStruct((n_heads, 2, t, t), F32),
        compiler_params=_params("arbitrary"),
        name="bias_tiles",
    )(idx, rel_bias)


def _attn_kernel(q_ref, k_ref, v_ref, bias_ref, rel_ref, lq1_ref, lk1_ref, lq2_ref, lk2_ref, sw_ref,
                 o_ref, vext_ref, *, lambda_init):
    h = pl.program_id(1)
    t = ATTN_BLOCK
    dh = DIFF_HEAD_DIM
    hw = 2 * dh
    seq = q_ref.shape[0]

    vext_ref[:, 0:hw] = v_ref[...]
    vext_ref[:, hw:] = jnp.ones((seq, hw), BF16)

    lam = (jnp.exp(jnp.sum(lq1_ref[...] * lk1_ref[...], axis=-1, keepdims=True))
           - jnp.exp(jnp.sum(lq2_ref[...] * lk2_ref[...], axis=-1, keepdims=True))
           + lambda_init)
    far_bias = rel_ref[NUM_BUCKETS - 1, h]
    diag = bias_ref[0]
    diag2 = jnp.concatenate([diag, diag], axis=0)
    left = bias_ref[1]
    left2 = jnp.concatenate([left, left], axis=0)

    def scores(i):
        q = q_ref[i * t:(i + 1) * t, :]
        lane = lax.broadcasted_iota(jnp.int32, q.shape, 1)
        zero = jnp.zeros_like(q)
        qq = jnp.concatenate([jnp.where(lane < dh, q, zero), jnp.where(lane >= dh, q, zero)], axis=0)
        return lax.dot_general(qq, k_ref[0:(i + 1) * t, :], (((1,), (1,)), ((), ())),
                               preferred_element_type=F32)

    n_q = seq // t
    s_next = scores(0)
    for i in range(n_q):
        n_kv = (i + 1) * t
        s = s_next
        if i + 1 < n_q:
            s_next = scores(i + 1)
        s_diag = s[:, n_kv - t:] + diag2
        m = jnp.max(s_diag, axis=-1, keepdims=True)
        parts = []
        if i >= 1:
            s_left = s[:, n_kv - 2 * t:n_kv - t] + left2
            m = jnp.maximum(m, jnp.max(s_left, axis=-1, keepdims=True))
        if i >= 2:
            s_far = s[:, 0:n_kv - 2 * t]
            m = jnp.maximum(m, jnp.max(s_far, axis=-1, keepdims=True) + far_bias)
            parts.append(jnp.exp(s_far - (m - far_bias)).astype(BF16))
        if i >= 1:
            parts.append(jnp.exp(s_left - m).astype(BF16))
        parts.append(jnp.exp(s_diag - m).astype(BF16))
        p = parts[0] if len(parts) == 1 else jnp.concatenate(parts, axis=1)
        acc = _dot(p, vext_ref[0:n_kv, :])
        out = acc[:, 0:hw] / acc[:, hw:]
        o = out[0:t, :] - lam * out[t:, :]
        o = _rms_scale(o, SUBLN_EPS) * sw_ref[...] * (1.0 - lambda_init)
        o_ref[i * t:(i + 1) * t, :] = o.astype(o_ref.dtype)


def _attention(qkv, bias_tiles, rel_bias, lq1, lk1, lq2, lk2, subln_w, *, bsz, seq, n_heads, lambda_init):
    t = ATTN_BLOCK
    hw = 2 * DIFF_HEAD_DIM
    qkv3 = qkv.reshape(bsz, seq, 3 * n_heads * hw)
    kernel = functools.partial(_attn_kernel, lambda_init=lambda_init)
    vec = lambda b, h: (0, 0)
    return pl.pallas_call(
        kernel,
        grid=(bsz, n_heads),
        in_specs=[
            pl.BlockSpec((None, seq, hw), lambda b, h: (b, 0, h)),
            pl.BlockSpec((None, seq, hw), lambda b, h: (b, 0, n_heads + h)),
            pl.BlockSpec((None, seq, hw), lambda b, h: (b, 0, 2 * n_heads + h)),
            pl.BlockSpec((None, 2, t, t), lambda b, h: (h, 0, 0, 0)),
            pl.BlockSpec(memory_space=pltpu.SMEM),
            pl.BlockSpec((1, DIFF_HEAD_DIM), vec),
            pl.BlockSpec((1, DIFF_HEAD_DIM), vec),
            pl.BlockSpec((1, DIFF_HEAD_DIM), vec),
            pl.BlockSpec((1, DIFF_HEAD_DIM), vec),
            pl.BlockSpec((1, hw), vec),
        ],
        out_specs=pl.BlockSpec((None, seq, hw), lambda b, h: (b, 0, h)),
        out_shape=jax.ShapeDtypeStruct((bsz, seq, n_heads * hw), BF16),
        scratch_shapes=[pltpu.VMEM((seq, 2 * hw), BF16)],
        compiler_params=_params("arbitrary", "arbitrary"),
        name="diff_attention",
    )(qkv3, qkv3, qkv3, bias_tiles, rel_bias, lq1[None, :], lk1[None, :], lq2[None, :], lk2[None, :],
      subln_w[None, :])


def kernel(x, a_in_proj, a_conv_w, a_conv_b, a_dt_bias, a_A_log, a_D, a_gate_norm_w, a_out_proj, kv_norm_w, w_k, w_v, rel_bias, b_w_q, b_lambda_q1, b_lambda_k1, b_lambda_q2, b_lambda_k2, b_subln_w, b_out_proj, norm_pre_mix, norm_post_mix, norm_pre_mlp, norm_post_mlp, mlp_w1, mlp_w2):
    bsz, seq, d = x.shape
    m = bsz * seq
    d_inner = a_out_proj.shape[1]
    n_ssm_heads = a_A_log.shape[1]
    n_zx = 2 * d_inner + 2 * N_GROUPS * D_STATE
    n_heads = d // (2 * DIFF_HEAD_DIM)
    x2d = x.reshape(m, d)

    tm = min(1024, m)
    tm_mlp = min(512, m)

    w_in = a_in_proj[0].astype(BF16)
    w_dt = jnp.pad(w_in[:, n_zx:], ((0, 0), (0, LANES - n_ssm_heads)))
    dt_bias = jnp.pad(a_dt_bias[0], (0, LANES - n_ssm_heads))[None, :]
    zx, dt = _in_proj(x2d, norm_pre_mix[0][None, :], w_in, w_dt, dt_bias,
                      n=n_zx, tm=tm, tn=min(1024, d_inner))
    g = _ssd(zx, dt, a_conv_w[0], a_conv_b[0], a_A_log[0], a_D[0], a_gate_norm_w[0],
             bsz=bsz, seq=seq, d_inner=d_inner)
    h = _out_proj(g.reshape(m, d_inner), a_out_proj[0].astype(BF16), x2d, norm_post_mix[0][None, :],
                  tm=tm_mlp, tk=min(1024, d_inner))
    h = _mlp(h, norm_pre_mlp[0][None, :], mlp_w1[0].astype(BF16), mlp_w2[0].astype(BF16),
             norm_post_mlp[0][None, :], tm=tm_mlp, tf=min(1024, mlp_w1.shape[2]))

    scale = DIFF_HEAD_DIM ** -0.5
    w_qkv = jnp.concatenate([b_w_q[0] * scale, w_k, w_v], axis=1).astype(BF16)
    qkv = _qkv(h, norm_pre_mix[1][None, :], kv_norm_w[None, :], w_qkv, tm=tm, tn=min(1024, d))
    lambda_init = 0.8 - 0.6 * math.exp(-0.3 * 1)
    bias = _bias_tiles(rel_bias, n_heads, ATTN_BLOCK)
    o = _attention(qkv, bias, rel_bias, b_lambda_q1[0], b_lambda_k1[0], b_lambda_q2[0], b_lambda_k2[0],
                   b_subln_w[0], bsz=bsz, seq=seq, n_heads=n_heads, lambda_init=lambda_init)
    h = _out_proj(o.reshape(m, d), b_out_proj[0].astype(BF16), h, norm_post_mix[1][None, :],
                  tm=tm_mlp, tk=min(1024, d))
    h = _mlp(h, norm_pre_mlp[1][None, :], mlp_w1[1].astype(BF16), mlp_w2[1].astype(BF16),
             norm_post_mlp[1][None, :], tm=tm_mlp, tf=min(1024, mlp_w1.shape[2]))
    return h.reshape(bsz, seq, d)
```

```python
import functools
import math

import numpy as np
import jax
import jax.numpy as jnp
from jax import lax
from jax.experimental import pallas as pl
from jax.experimental.pallas import tpu as pltpu

SSM_HEAD_DIM = 64
D_STATE = 128
N_GROUPS = 8
CONV_W = 4
CHUNK = 128
DIFF_HEAD_DIM = 64
NUM_BUCKETS = 32
MAX_DISTANCE = 128
NORM_EPS = 1e-6
GATE_NORM_EPS = 1e-5
SUBLN_EPS = 1e-5
LOG2_E = math.log2(math.e)

LANES = 128
CONV_TAIL_ROWS = 8
ATTN_BLOCK = 256
VMEM_LIMIT_BYTES = 56 * 1024 * 1024

BF16 = jnp.bfloat16
F32 = jnp.float32


def _params(*sem):
    return pltpu.CompilerParams(dimension_semantics=sem, vmem_limit_bytes=VMEM_LIMIT_BYTES)


def _rms_scale(xf, eps):
    return xf * lax.rsqrt(jnp.mean(xf * xf, axis=-1, keepdims=True) + eps)


def _silu(v):
    return v * (1.0 / (1.0 + jnp.exp(-v)))


def _dot(a, b):
    return jnp.dot(a, b, preferred_element_type=F32)


def _in_proj_kernel(x_ref, nw_ref, w_ref, wdt_ref, dtb_ref, zx_ref, dt_ref, u_ref):
    j = pl.program_id(1)

    @pl.when(j == 0)
    def _():
        xf = x_ref[...]
        u = (_rms_scale(xf, NORM_EPS) * nw_ref[...]).astype(BF16)
        u_ref[...] = u
        raw = _dot(u, wdt_ref[...]) + dtb_ref[...]
        dt_ref[...] = jnp.maximum(raw, 0.0) + jnp.log1p(jnp.exp(-jnp.abs(raw)))

    zx_ref[...] = _dot(u_ref[...], w_ref[...]).astype(zx_ref.dtype)


def _in_proj(x2d, norm_w, w_zx, w_dt, dt_bias, *, n, tm, tn):
    m, d = x2d.shape
    assert n % tn == 0
    return pl.pallas_call(
        _in_proj_kernel,
        grid=(m // tm, n // tn),
        in_specs=[
            pl.BlockSpec((tm, d), lambda i, j: (i, 0)),
            pl.BlockSpec((1, d), lambda i, j: (0, 0)),
            pl.BlockSpec((d, tn), lambda i, j: (0, j)),
            pl.BlockSpec((d, LANES), lambda i, j: (0, 0)),
            pl.BlockSpec((1, LANES), lambda i, j: (0, 0)),
        ],
        out_specs=[
            pl.BlockSpec((tm, tn), lambda i, j: (i, j)),
            pl.BlockSpec((tm, LANES), lambda i, j: (i, 0)),
        ],
        out_shape=[
            jax.ShapeDtypeStruct((m, n), F32),
            jax.ShapeDtypeStruct((m, LANES), F32),
        ],
        scratch_shapes=[pltpu.VMEM((tm, d), BF16)],
        compiler_params=_params("arbitrary", "arbitrary"),
        name="in_proj",
    )(x2d, norm_w, w_zx, w_dt, dt_bias)


def _split3(v):
    hi = v.astype(BF16)
    r1 = v - hi.astype(F32)
    mid = r1.astype(BF16)
    lo = (r1 - mid.astype(F32)).astype(BF16)
    return hi, mid, lo


def _ssd_kernel(z_ref, x_ref, bc_ref, dt_ref, cwx_ref, cbx_ref, cwbc_ref, cbbc_ref,
                alog_ref, dskip_ref, gnw_ref, g_ref,
                state_ref, xtail_ref, bctail_ref, cumT_ref, dtT_ref, wT_ref, *, heads_per_group):
    c = pl.program_id(1)
    q = CHUNK
    gp = heads_per_group * SSM_HEAD_DIM
    n_bc = N_GROUPS * D_STATE

    @pl.when(c == 0)
    def _():
        state_ref[...] = jnp.zeros_like(state_ref)
        xtail_ref[...] = jnp.zeros_like(xtail_ref)
        bctail_ref[...] = jnp.zeros_like(bctail_ref)

    dt = dt_ref[...]
    a = dt * (-jnp.exp(alog_ref[...]))
    row = lax.broadcasted_iota(jnp.int32, (q, q), 0)
    col = lax.broadcasted_iota(jnp.int32, (q, q), 1)
    tril = (col <= row).astype(BF16)
    a_hi, a_mid, a_lo = _split3(a)
    cum = _dot(tril, a_hi) + _dot(tril, a_mid) + _dot(tril, a_lo)
    cumT = cum.T * LOG2_E
    dtT = dt.T
    cum_last = cumT[:, q - 1:q]
    cumT_ref[...] = cumT
    dtT_ref[...] = dtT
    wT_ref[...] = jnp.exp2(cum_last - cumT) * dtT

    causal_T = row <= col

    def conv_silu(tail_ref, cur_ref, w_ref, b_ref, start, width):
        cols = slice(start, start + width)
        ext = jnp.concatenate([tail_ref[:, cols], cur_ref[:, cols]], axis=0)
        w = w_ref[:, cols]
        acc = b_ref[:, cols] + w[CONV_W - 1:CONV_W, :] * ext[CONV_TAIL_ROWS:, :]
        for s in range(1, CONV_W):
            shifted = pltpu.roll(ext, s, axis=0)[CONV_TAIL_ROWS:, :]
            acc = acc + w[CONV_W - 1 - s:CONV_W - s, :] * shifted
        return _silu(acc)

    for g in range(N_GROUPS):
        xs = conv_silu(xtail_ref, x_ref, cwx_ref, cbx_ref, g * gp, gp)
        bg16 = conv_silu(bctail_ref, bc_ref, cwbc_ref, cbbc_ref, g * D_STATE, D_STATE).astype(BF16)
        cgT16 = conv_silu(bctail_ref, bc_ref, cwbc_ref, cbbc_ref,
                          n_bc + g * D_STATE, D_STATE).T.astype(BF16)
        cbT = _dot(bg16, cgT16)
        xT = xs.T
        h0 = g * heads_per_group
        cum_g = cumT_ref[h0:h0 + heads_per_group, :]
        dt_g = dtT_ref[h0:h0 + heads_per_group, :]
        w_g = wT_ref[h0:h0 + heads_per_group, :]
        st = state_ref[g]
        inter = _dot(st.astype(BF16), cgT16)
        y_parts = []
        xw_parts = []
        dec_parts = []
        for r in range(heads_per_group):
            lo, hi = r * SSM_HEAD_DIM, (r + 1) * SSM_HEAD_DIM
            cum_row = cum_g[r:r + 1, :]
            cum_col = jnp.broadcast_to(cum_row, (q, q)).T
            seg = jnp.where(causal_T, cum_row - cum_col, -jnp.inf)
            mT = (cbT * jnp.exp2(seg)).astype(BF16)
            xh = xT[lo:hi, :]
            xdt = (xh * dt_g[r:r + 1, :]).astype(BF16)
            y = _dot(xdt, mT)
            y = y + inter[lo:hi, :] * jnp.exp2(cum_row)
            y = y + dskip_ref[h0 + r] * xh
            y_parts.append(y)
            xw_parts.append((xh * w_g[r:r + 1, :]).astype(BF16))
            dec = jnp.exp2(cum_row[:, q - 1:q])
            dec_parts.append(jnp.broadcast_to(dec, (SSM_HEAD_DIM, D_STATE)))
        yT = jnp.concatenate(y_parts, axis=0)
        xw = jnp.concatenate(xw_parts, axis=0)
        decay = jnp.concatenate(dec_parts, axis=0)
        state_ref[g] = st * decay + _dot(xw, bg16)
        gated = yT.T * _silu(z_ref[:, g * gp:(g + 1) * gp])
        gated = _rms_scale(gated, GATE_NORM_EPS) * gnw_ref[:, g * gp:(g + 1) * gp]
        g_ref[:, g * gp:(g + 1) * gp] = gated.astype(g_ref.dtype)

    xtail_ref[...] = x_ref[q - CONV_TAIL_ROWS:, :]
    bctail_ref[...] = bc_ref[q - CONV_TAIL_ROWS:, :]


def _ssd(zx, dt, conv_w, conv_b, a_log, d_skip, gate_norm_w, *, bsz, seq, d_inner):
    n_heads = d_inner // SSM_HEAD_DIM
    hpg = n_heads // N_GROUPS
    n_bc = N_GROUPS * D_STATE
    nc = seq // CHUNK
    zx3 = zx.reshape(bsz, seq, zx.shape[-1])
    dt3 = dt.reshape(bsz, seq, LANES)
    blk_x = d_inner // (2 * n_bc)
    assert d_inner % (2 * n_bc) == 0
    cw_x, cw_bc = conv_w[:, :d_inner], conv_w[:, d_inner:]
    cb_x, cb_bc = conv_b[None, :d_inner], conv_b[None, d_inner:]
    alog = jnp.pad(a_log, (0, LANES - n_heads))[None, :]
    kernel = functools.partial(_ssd_kernel, heads_per_group=hpg)
    const2 = lambda b, c: (0, 0)
    return pl.pallas_call(
        kernel,
        grid=(bsz, nc),
        in_specs=[
            pl.BlockSpec((None, CHUNK, d_inner), lambda b, c: (b, c, 0)),
            pl.BlockSpec((None, CHUNK, d_inner), lambda b, c: (b, c, 1)),
            pl.BlockSpec((None, CHUNK, 2 * n_bc), lambda b, c: (b, c, 2 * blk_x)),
            pl.BlockSpec((None, CHUNK, LANES), lambda b, c: (b, c, 0)),
            pl.BlockSpec((CONV_W, d_inner), const2),
            pl.BlockSpec((1, d_inner), const2),
            pl.BlockSpec((CONV_W, 2 * n_bc), const2),
            pl.BlockSpec((1, 2 * n_bc), const2),
            pl.BlockSpec((1, LANES), const2),
            pl.BlockSpec(memory_space=pltpu.SMEM),
            pl.BlockSpec((1, d_inner), const2),
        ],
        out_specs=pl.BlockSpec((None, CHUNK, d_inner), lambda b, c: (b, c, 0)),
        out_shape=jax.ShapeDtypeStruct((bsz, seq, d_inner), BF16),
        scratch_shapes=[
            pltpu.VMEM((N_GROUPS, hpg * SSM_HEAD_DIM, D_STATE), F32),
            pltpu.VMEM((CONV_TAIL_ROWS, d_inner), F32),
            pltpu.VMEM((CONV_TAIL_ROWS, 2 * n_bc), F32),
            pltpu.VMEM((LANES, CHUNK), F32),
            pltpu.VMEM((LANES, CHUNK), F32),
            pltpu.VMEM((LANES, CHUNK), F32),
        ],
        compiler_params=_params("arbitrary", "arbitrary"),
        name="ssd",
    )(zx3, zx3, zx3, dt3, cw_x, cb_x, cw_bc, cb_bc, alog, d_skip, gate_norm_w[None, :])


def _out_proj_kernel(a_ref, w_ref, res_ref, nw_ref, o_ref, acc0_ref, acc1_ref):
    i = pl.program_id(0)
    n_tiles = pl.num_programs(0) - 1
    accs = (acc0_ref, acc1_ref)

    def epilogue(acc_ref):
        o_ref[...] = res_ref[...] + _rms_scale(acc_ref[...], NORM_EPS) * nw_ref[...]

    def matmul(acc_ref):
        acc_ref[...] = _dot(a_ref[...], w_ref[...])

    @pl.when(i == 0)
    def _():
        matmul(accs[0])

    for parity in range(2):
        @pl.when(jnp.logical_and(jnp.logical_and(i > 0, i < n_tiles), i % 2 == parity))
        def _():
            epilogue(accs[1 - parity])
            matmul(accs[parity])

    for parity in range(2):
        @pl.when(jnp.logical_and(i == n_tiles, i % 2 == parity))
        def _():
            epilogue(accs[1 - parity])


def _out_proj(a, w, res, norm_w, *, tm):
    m, kdim = a.shape
    d = w.shape[1]
    n_tiles = m // tm
    cur = lambda i: (jnp.minimum(i, n_tiles - 1), 0)
    prev = lambda i: (jnp.maximum(i - 1, 0), 0)
    return pl.pallas_call(
        _out_proj_kernel,
        grid=(n_tiles + 1,),
        in_specs=[
            pl.BlockSpec((tm, kdim), cur),
            pl.BlockSpec((kdim, d), lambda i: (0, 0), pipeline_mode=pl.Buffered(1)),
            pl.BlockSpec((tm, d), prev),
            pl.BlockSpec((1, d), lambda i: (0, 0)),
        ],
        out_specs=pl.BlockSpec((tm, d), prev),
        out_shape=jax.ShapeDtypeStruct((m, d), F32),
        scratch_shapes=[pltpu.VMEM((tm, d), F32), pltpu.VMEM((tm, d), F32)],
        compiler_params=_params("arbitrary"),
        name="out_proj",
    )(a, w, res, norm_w)


def _mlp_kernel(h_ref, pre_ref, w1_ref, w2_ref, post_ref, o_ref, u_ref):
    f = pl.program_id(1)

    @pl.when(f == 0)
    def _():
        u_ref[...] = (_rms_scale(h_ref[...], NORM_EPS) * pre_ref[...]).astype(BF16)
        o_ref[...] = jnp.zeros_like(o_ref)

    a = jnp.maximum(_dot(u_ref[...], w1_ref[...]), 0.0)
    o_ref[...] += _dot((a * a).astype(BF16), w2_ref[...])

    @pl.when(f == pl.num_programs(1) - 1)
    def _():
        o_ref[...] = h_ref[...] + _rms_scale(o_ref[...], NORM_EPS) * post_ref[...]


def _mlp(h, pre_w, w1, w2, post_w, *, tm, tf):
    m, d = h.shape
    ff = w1.shape[1]
    return pl.pallas_call(
        _mlp_kernel,
        grid=(m // tm, ff // tf),
        in_specs=[
            pl.BlockSpec((tm, d), lambda i, f: (i, 0)),
            pl.BlockSpec((1, d), lambda i, f: (0, 0)),
            pl.BlockSpec((d, tf), lambda i, f: (0, f)),
            pl.BlockSpec((tf, d), lambda i, f: (f, 0)),
            pl.BlockSpec((1, d), lambda i, f: (0, 0)),
        ],
        out_specs=pl.BlockSpec((tm, d), lambda i, f: (i, 0)),
        out_shape=jax.ShapeDtypeStruct((m, d), F32),
        scratch_shapes=[pltpu.VMEM((tm, d), BF16)],
        compiler_params=_params("arbitrary", "arbitrary"),
        name="mlp",
    )(h, pre_w, w1, w2, post_w)


def _qkv_kernel(h_ref, qn_ref, kvn_ref, w_ref, o_ref, u_ref, *, n_q_tiles):
    j = pl.program_id(1)

    @pl.when(j == 0)
    def _():
        xhat = _rms_scale(h_ref[...], NORM_EPS)
        u_ref[0] = (xhat * qn_ref[...]).astype(BF16)
        u_ref[1] = (xhat * kvn_ref[...]).astype(BF16)

    sel = (j >= n_q_tiles).astype(jnp.int32)
    o_ref[...] = _dot(u_ref[sel], w_ref[...]).astype(o_ref.dtype)


def _qkv(h, q_norm_w, kv_norm_w, w_qkv, *, tm, tn):
    m, d = h.shape
    n = w_qkv.shape[1]
    kernel = functools.partial(_qkv_kernel, n_q_tiles=(n // 3) // tn)
    return pl.pallas_call(
        kernel,
        grid=(m // tm, n // tn),
        in_specs=[
            pl.BlockSpec((tm, d), lambda i, j: (i, 0)),
            pl.BlockSpec((1, d), lambda i, j: (0, 0)),
            pl.BlockSpec((1, d), lambda i, j: (0, 0)),
            pl.BlockSpec((d, tn), lambda i, j: (0, j)),
        ],
        out_specs=pl.BlockSpec((tm, tn), lambda i, j: (i, j)),
        out_shape=jax.ShapeDtypeStruct((m, n), BF16),
        scratch_shapes=[pltpu.VMEM((2, tm, d), BF16)],
        compiler_params=_params("arbitrary", "arbitrary"),
        name="qkv",
    )(h, q_norm_w, kv_norm_w, w_qkv)


def _t5_bucket_np(dist):
    n = np.maximum(dist, 0)
    max_exact = NUM_BUCKETS // 2
    nf = np.maximum(n, 1).astype(np.float32)
    large = max_exact + (np.log(nf / np.float32(max_exact)) / np.float32(math.log(MAX_DISTANCE / max_exact))
                         * np.float32(NUM_BUCKETS - max_exact)).astype(np.int32)
    large = np.minimum(large, NUM_BUCKETS - 1)
    return np.where(n < max_exact, n, large).astype(np.int32)


def _bucket_tiles(t):
    i = np.arange(t)[:, None]
    j = np.arange(t)[None, :]
    diag = np.where(i >= j, _t5_bucket_np(i - j), -1)
    left = _t5_bucket_np(t + i - j)
    return np.stack([diag, left]).astype(np.int32)


def _bias_kernel(idx_ref, rel_ref, o_ref):
    h = pl.program_id(0)
    idx = idx_ref[...]
    acc = jnp.where(idx < 0, -jnp.inf, 0.0).astype(F32)
    for b in range(NUM_BUCKETS):
        acc = jnp.where(idx == b, rel_ref[b, h], acc)
    o_ref[...] = acc


def _bias_tiles(rel_bias, n_heads, t):
    idx = jnp.asarray(_bucket_tiles(t))
    return pl.pallas_call(
        _bias_kernel,
        grid=(n_heads,),
        in_specs=[
            pl.BlockSpec((2, t, t), lambda h: (0, 0, 0)),
            pl.BlockSpec(memory_space=pltpu.SMEM),
        ],
        out_specs=pl.BlockSpec((None, 2, t, t), lambda h: (h, 0, 0, 0)),
        out_shape=jax.ShapeDtypeStruct((n_heads, 2, t, t), F32),
        compiler_params=_params("arbitrary"),
        name="bias_tiles",
    )(idx, rel_bias)


def _attn_kernel(q_ref, k_ref, v_ref, bias_ref, rel_ref, lq1_ref, lk1_ref, lq2_ref, lk2_ref, sw_ref,
                 o_ref, vext_ref, *, lambda_init):
    h = pl.program_id(1)
    t = ATTN_BLOCK
    dh = DIFF_HEAD_DIM
    hw = 2 * dh
    seq = q_ref.shape[0]

    vext_ref[:, 0:hw] = v_ref[...]
    vext_ref[:, hw:] = jnp.ones((seq, hw), BF16)

    lam = (jnp.exp(jnp.sum(lq1_ref[...] * lk1_ref[...], axis=-1, keepdims=True))
           - jnp.exp(jnp.sum(lq2_ref[...] * lk2_ref[...], axis=-1, keepdims=True))
           + lambda_init)
    far_bias = rel_ref[NUM_BUCKETS - 1, h]
    diag = bias_ref[0]
    diag2 = jnp.concatenate([diag, diag], axis=0)
    left = bias_ref[1]
    left2 = jnp.concatenate([left, left], axis=0)

    def scores(i):
        q = q_ref[i * t:(i + 1) * t, :]
        lane = lax.broadcasted_iota(jnp.int32, q.shape, 1)
        zero = jnp.zeros_like(q)
        qq = jnp.concatenate([jnp.where(lane < dh, q, zero), jnp.where(lane >= dh, q, zero)], axis=0)
        return lax.dot_general(qq, k_ref[0:(i + 1) * t, :], (((1,), (1,)), ((), ())),
                               preferred_element_type=F32)

    n_q = seq // t
    s_next = scores(0)
    for i in range(n_q):
        n_kv = (i + 1) * t
        s = s_next
        if i + 1 < n_q:
            s_next = scores(i + 1)
        s_diag = s[:, n_kv - t:] + diag2
        m = jnp.max(s_diag, axis=-1, keepdims=True)
        parts = []
        if i >= 1:
            s_left = s[:, n_kv - 2 * t:n_kv - t] + left2
            m = jnp.maximum(m, jnp.max(s_left, axis=-1, keepdims=True))
        if i >= 2:
            s_far = s[:, 0:n_kv - 2 * t]
            m = jnp.maximum(m, jnp.max(s_far, axis=-1, keepdims=True) + far_bias)
            parts.append(jnp.exp(s_far - (m - far_bias)).astype(BF16))
        if i >= 1:
            parts.append(jnp.exp(s_left - m).astype(BF16))
        parts.append(jnp.exp(s_diag - m).astype(BF16))
        p = parts[0] if len(parts) == 1 else jnp.concatenate(parts, axis=1)
        acc = _dot(p, vext_ref[0:n_kv, :])
        out = acc[:, 0:hw] / acc[:, hw:]
        o = out[0:t, :] - lam * out[t:, :]
        o = _rms_scale(o, SUBLN_EPS) * sw_ref[...] * (1.0 - lambda_init)
        o_ref[i * t:(i + 1) * t, :] = o.astype(o_ref.dtype)


def _attention(qkv, bias_tiles, rel_bias, lq1, lk1, lq2, lk2, subln_w, *, bsz, seq, n_heads, lambda_init):
    t = ATTN_BLOCK
    hw = 2 * DIFF_HEAD_DIM
    qkv3 = qkv.reshape(bsz, seq, 3 * n_heads * hw)
    kernel = functools.partial(_attn_kernel, lambda_init=lambda_init)
    vec = lambda b, h: (0, 0)
    return pl.pallas_call(
        kernel,
        grid=(bsz, n_heads),
        in_specs=[
            pl.BlockSpec((None, seq, hw), lambda b, h: (b, 0, h)),
            pl.BlockSpec((None, seq, hw), lambda b, h: (b, 0, n_heads + h)),
            pl.BlockSpec((None, seq, hw), lambda b, h: (b, 0, 2 * n_heads + h)),
            pl.BlockSpec((None, 2, t, t), lambda b, h: (h, 0, 0, 0)),
            pl.BlockSpec(memory_space=pltpu.SMEM),
            pl.BlockSpec((1, DIFF_HEAD_DIM), vec),
            pl.BlockSpec((1, DIFF_HEAD_DIM), vec),
            pl.BlockSpec((1, DIFF_HEAD_DIM), vec),
            pl.BlockSpec((1, DIFF_HEAD_DIM), vec),
            pl.BlockSpec((1, hw), vec),
        ],
        out_specs=pl.BlockSpec((None, seq, hw), lambda b, h: (b, 0, h)),
        out_shape=jax.ShapeDtypeStruct((bsz, seq, n_heads * hw), BF16),
        scratch_shapes=[pltpu.VMEM((seq, 2 * hw), BF16)],
        compiler_params=_params("arbitrary", "arbitrary"),
        name="diff_attention",
    )(qkv3, qkv3, qkv3, bias_tiles, rel_bias, lq1[None, :], lk1[None, :], lq2[None, :], lk2[None, :],
      subln_w[None, :])


def kernel(x, a_in_proj, a_conv_w, a_conv_b, a_dt_bias, a_A_log, a_D, a_gate_norm_w, a_out_proj, kv_norm_w, w_k, w_v, rel_bias, b_w_q, b_lambda_q1, b_lambda_k1, b_lambda_q2, b_lambda_k2, b_subln_w, b_out_proj, norm_pre_mix, norm_post_mix, norm_pre_mlp, norm_post_mlp, mlp_w1, mlp_w2):
    bsz, seq, d = x.shape
    m = bsz * seq
    d_inner = a_out_proj.shape[1]
    n_ssm_heads = a_A_log.shape[1]
    n_zx = 2 * d_inner + 2 * N_GROUPS * D_STATE
    n_heads = d // (2 * DIFF_HEAD_DIM)
    x2d = x.reshape(m, d)

    tm = min(1024, m)
    tm_mlp = min(512, m)

    w_in = a_in_proj[0].astype(BF16)
    w_dt = jnp.pad(w_in[:, n_zx:], ((0, 0), (0, LANES - n_ssm_heads)))
    dt_bias = jnp.pad(a_dt_bias[0], (0, LANES - n_ssm_heads))[None, :]
    zx, dt = _in_proj(x2d, norm_pre_mix[0][None, :], w_in, w_dt, dt_bias,
                      n=n_zx, tm=tm, tn=min(1024, d_inner))
    g = _ssd(zx, dt, a_conv_w[0], a_conv_b[0], a_A_log[0], a_D[0], a_gate_norm_w[0],
             bsz=bsz, seq=seq, d_inner=d_inner)
    h = _out_proj(g.reshape(m, d_inner), a_out_proj[0].astype(BF16), x2d, norm_post_mix[0][None, :],
                  tm=tm_mlp)
    h = _mlp(h, norm_pre_mlp[0][None, :], mlp_w1[0].astype(BF16), mlp_w2[0].astype(BF16),
             norm_post_mlp[0][None, :], tm=tm_mlp, tf=min(1024, mlp_w1.shape[2]))

    scale = DIFF_HEAD_DIM ** -0.5
    w_qkv = jnp.concatenate([b_w_q[0] * scale, w_k, w_v], axis=1).astype(BF16)
    qkv = _qkv(h, norm_pre_mix[1][None, :], kv_norm_w[None, :], w_qkv, tm=tm, tn=min(1024, d))
    lambda_init = 0.8 - 0.6 * math.exp(-0.3 * 1)
    bias = _bias_tiles(rel_bias, n_heads, ATTN_BLOCK)
    o = _attention(qkv, bias, rel_bias, b_lambda_q1[0], b_lambda_k1[0], b_lambda_q2[0], b_lambda_k2[0],
                   b_subln_w[0], bsz=bsz, seq=seq, n_heads=n_heads, lambda_init=lambda_init)
    h = _out_proj(o.reshape(m, d), b_out_proj[0].astype(BF16), h, norm_post_mix[1][None, :],
                  tm=tm_mlp)
    h = _mlp(h, norm_pre_mlp[1][None, :], mlp_w1[1].astype(BF16), mlp_w2[1].astype(BF16),
             norm_post_mlp[1][None, :], tm=tm_mlp, tf=min(1024, mlp_w1.shape[2]))
    return h.reshape(bsz, seq, d)
```
